```python
import math
import jax, jax.numpy as jnp
from jax import lax
import numpy as np

D_MODEL = 2048
BATCH = 16
SEQ = 2048
DEPTH = 2

HEAD_DIM = 64
BRANCH_WIDTH = 512
N_BRANCH = 4

RW_HEADS = 8
RW_WIDTH = RW_HEADS * HEAD_DIM
RW_DECAY_RANK = 64
RW_ICLR_RANK = 64
RW_GATE_RANK = 128
RW_COLS = 3 * RW_WIDTH + RW_DECAY_RANK + RW_ICLR_RANK + RW_GATE_RANK
RW_GN_EPS = 64e-5

GLA_HEADS = 4
GLA_DK = 64
GLA_DV = 128
GLA_QK_WIDTH = GLA_HEADS * GLA_DK
GLA_V_WIDTH = GLA_HEADS * GLA_DV
GLA_GATE_RANK = 16
GLA_GATE_NORM = 16.0
GLA_CHUNK = 64
GLA_COLS = 2 * GLA_QK_WIDTH + GLA_V_WIDTH + GLA_GATE_RANK + GLA_V_WIDTH

FOX_HEADS = 8
FOX_WIDTH = FOX_HEADS * HEAD_DIM
FOX_BLOCK = 128
FOX_COLS = 3 * FOX_WIDTH + FOX_HEADS

S5_GROUP = 16
S5_GROUPS = 32
S5_WIDTH = S5_GROUP * S5_GROUPS
S5_STATE = 64
S5_DT_MIN = 0.001
S5_DT_MAX = 0.1
S5_COLS = S5_WIDTH

IN_COLS = RW_COLS + GLA_COLS + FOX_COLS + S5_COLS
D_FF = 5632
FFN_RES = 0.5
NORM_EPS = 1e-6

kernel_name = "hybrid_macaron_rwkv7_gla_fox_s5"


def _split(z, sizes):
    offs, s = [], 0
    for n in sizes[:-1]:
        s += n
        offs.append(s)
    return jnp.split(z, offs, axis=-1)


def rmsnorm(x, g):
    xf = x.astype(jnp.float32)
    y = xf * lax.rsqrt(jnp.mean(xf * xf, axis=-1, keepdims=True) + NORM_EPS)
    return (y * g.astype(jnp.float32)).astype(x.dtype)


def swiglu(h, w_in, w_out):
    gate = h @ w_in[:, :D_FF]
    up = h @ w_in[:, D_FF:]
    return (jax.nn.silu(gate) * up) @ w_out


def token_shift(z, mu):
    z_prev = jnp.pad(z, ((0, 0), (1, 0), (0, 0)))[:, :-1]
    return z + (z_prev - z) * mu


def rwkv7_mixer(c, mu, w0, w2, a0, a2, g2, k_k, k_a, r_k, ln_g, ln_b):
    b, l, _ = c.shape
    f32 = jnp.float32
    c = token_shift(c, mu)
    r, k, v, xw, xa, xg = _split(c, [RW_WIDTH, RW_WIDTH, RW_WIDTH, RW_DECAY_RANK, RW_ICLR_RANK, RW_GATE_RANK])
    w_log = -jax.nn.softplus(-(w0 + jnp.tanh(xw) @ w2).astype(f32)) - 0.5
    decay = jnp.exp(-jnp.exp(w_log))
    a = jax.nn.sigmoid((a0 + xa @ a2).astype(f32))
    g = jax.nn.sigmoid(xg) @ g2

    def heads(t):
        return t.astype(f32).reshape(b, l, RW_HEADS, HEAD_DIM)

    r, k, v, decay, a = heads(r), heads(k), heads(v), heads(decay), heads(a)
    kk = k * k_k.astype(f32).reshape(RW_HEADS, HEAD_DIM)
    kk = kk / jnp.maximum(jnp.sqrt(jnp.sum(kk * kk, axis=-1, keepdims=True)), 1e-12)
    k = k * (1.0 + (a - 1.0) * k_a.astype(f32).reshape(RW_HEADS, HEAD_DIM))

    def step(state, inp):
        r_t, w_t, k_t, v_t, kk_t, a_t = inp
        sa = jnp.einsum('bhvk,bhk->bhv', state, -kk_t)
        state = (state * w_t[:, :, None, :]
                 + sa[..., None] * (kk_t * a_t)[:, :, None, :]
                 + v_t[..., None] * k_t[:, :, None, :])
        return state, jnp.einsum('bhvk,bhk->bhv', state, r_t)

    s0 = jnp.zeros((b, RW_HEADS, HEAD_DIM, HEAD_DIM), f32)
    seq_in = tuple(jnp.moveaxis(t, 1, 0) for t in (r, decay, k, v, kk, a))
    _, y = lax.scan(step, s0, seq_in)
    y = jnp.moveaxis(y, 0, 1)
    mean = jnp.mean(y, axis=-1, keepdims=True)
    var = jnp.mean(jnp.square(y - mean), axis=-1, keepdims=True)
    y = (y - mean) * lax.rsqrt(var + RW_GN_EPS)
    y = y * ln_g.astype(f32).reshape(RW_HEADS, HEAD_DIM) + ln_b.astype(f32).reshape(RW_HEADS, HEAD_DIM)
    y = y + jnp.sum(r * k * r_k.astype(f32), axis=-1, keepdims=True) * v
    return (y.reshape(b, l, RW_WIDTH) * g).astype(c.dtype)


def gla_mixer(c, w_g2, b_g2, norm_g):
    b, l, _ = c.shape
    f32 = jnp.float32
    n_c = l // GLA_CHUNK
    q, k, v, glr, og = _split(c, [GLA_QK_WIDTH, GLA_QK_WIDTH, GLA_V_WIDTH, GLA_GATE_RANK, GLA_V_WIDTH])
    log_a = jax.nn.log_sigmoid((glr @ w_g2 + b_g2).astype(f32)) / GLA_GATE_NORM

    def chunks(t, d):
        return t.astype(f32).reshape(b, n_c, GLA_CHUNK, GLA_HEADS, d).transpose(0, 3, 1, 2, 4)

    q = chunks(q, GLA_DK) * GLA_DK ** -0.5
    k = chunks(k, GLA_DK)
    v = chunks(v, GLA_DV)
    cum = jnp.cumsum(chunks(log_a, GLA_DK), axis=3)
    cum_last = cum[:, :, :, -1:, :]
    q_dec = q * jnp.exp(cum)
    k_inv = k * jnp.exp(-cum)
    k_end = k * jnp.exp(cum_last - cum)
    causal = jnp.tril(jnp.ones((GLA_CHUNK, GLA_CHUNK), dtype=bool))
    att = jnp.where(causal, jnp.einsum('bhntd,bhnsd->bhnts', q_dec, k_inv), 0.0)
    o = jnp.einsum('bhnts,bhnsv->bhntv', att, v)
    kv = jnp.einsum('bhnsd,bhnsv->bhndv', k_end, v)
    chunk_decay = jnp.exp(cum_last[:, :, :, 0, :])

    def step(state, inp):
        dec, kv_c = inp
        return state * dec[..., None] + kv_c, state

    s0 = jnp.zeros((b, GLA_HEADS, GLA_DK, GLA_DV), f32)
    _, s_in = lax.scan(step, s0, (jnp.moveaxis(chunk_decay, 2, 0), jnp.moveaxis(kv, 2, 0)))
    o = o + jnp.einsum('bhntd,nbhdv->bhntv', q_dec, s_in)
    o = o.transpose(0, 2, 3, 1, 4).reshape(b, l, GLA_HEADS, GLA_DV)
    o = rmsnorm(o, norm_g).reshape(b, l, GLA_V_WIDTH)
    return (o * jax.nn.silu(og.astype(f32))).astype(c.dtype)


def fox_mixer(c, b_f, q_g, k_g):
    b, l, _ = c.shape
    f32 = jnp.float32
    q, k, v, f_logit = _split(c, [FOX_WIDTH, FOX_WIDTH, FOX_WIDTH, FOX_HEADS])

    def heads(t):
        return t.reshape(b, l, FOX_HEADS, HEAD_DIM).transpose(0, 2, 1, 3)

    q = rmsnorm(heads(q), q_g)
    k = rmsnorm(heads(k), k_g)
    v = heads(v)
    log_f = jax.nn.log_sigmoid((f_logit + b_f).astype(f32))
    cum_f = jnp.cumsum(log_f, axis=1).transpose(0, 2, 1)
    scale = HEAD_DIM ** -0.5
    outs = []
    for start in range(0, l, FOX_BLOCK):
        end = start + FOX_BLOCK
        logits = jnp.einsum('bhqd,bhkd->bhqk', q[:, :, start:end], k[:, :, :end]).astype(f32) * scale
        logits = logits + cum_f[:, :, start:end, None] - cum_f[:, :, None, :end]
        causal = (start + jnp.arange(FOX_BLOCK))[:, None] >= jnp.arange(end)[None, :]
        p = jax.nn.softmax(jnp.where(causal, logits, -jnp.inf), axis=-1)
        outs.append(jnp.einsum('bhqk,bhkd->bhqd', p.astype(v.dtype), v[:, :, :end]))
    o = jnp.concatenate(outs, axis=2)
    return o.transpose(0, 2, 1, 3).reshape(b, l, FOX_WIDTH)


def _complex_affine_combine(e1, e2):
    a1r, a1i, b1r, b1i = e1
    a2r, a2i, b2r, b2i = e2
    ar = a2r * a1r - a2i * a1i
    ai = a2r * a1i + a2i * a1r
    br = a2r * b1r - a2i * b1i + b2r
    bi = a2r * b1i + a2i * b1r + b2i
    return ar, ai, br, bi


def s5_mixer(u, lam_re, lam_im, log_dt, b_re, b_im, c_re, c_im, d_skip, w_glu, b_glu):
    b, l, _ = u.shape
    f32 = jnp.float32
    uf = u.astype(f32).reshape(b, l, S5_GROUPS, S5_GROUP)
    lr = lam_re.astype(f32)
    li = lam_im.astype(f32)
    dt = jnp.exp(log_dt.astype(f32))[:, None]
    mag = jnp.exp(lr * dt)
    ab_re = mag * jnp.cos(li * dt)
    ab_im = mag * jnp.sin(li * dt)
    den = lr * lr + li * li
    z_re = ((ab_re - 1.0) * lr + ab_im * li) / den
    z_im = (ab_im * lr - (ab_re - 1.0) * li) / den
    br = b_re.astype(f32)
    bi = b_im.astype(f32)
    bb_re = z_re[..., None] * br - z_im[..., None] * bi
    bb_im = z_re[..., None] * bi + z_im[..., None] * br
    bu_re = jnp.einsum('blgh,gph->blgp', uf, bb_re)
    bu_im = jnp.einsum('blgh,gph->blgp', uf, bb_im)
    a_re = jnp.broadcast_to(ab_re, (1, l, S5_GROUPS, S5_STATE))
    a_im = jnp.broadcast_to(ab_im, (1, l, S5_GROUPS, S5_STATE))
    _, _, x_re, x_im = lax.associative_scan(_complex_affine_combine, (a_re, a_im, bu_re, bu_im), axis=1)
    y = (jnp.einsum('blgp,ghp->blgh', x_re, c_re.astype(f32))
         - jnp.einsum('blgp,ghp->blgh', x_im, c_im.astype(f32))
         + d_skip.astype(f32) * uf)
    y = jax.nn.gelu(y.reshape(b, l, S5_WIDTH))
    y = y * jax.nn.sigmoid(y @ w_glu.astype(f32) + b_glu.astype(f32))
    return y.astype(u.dtype)


def setup_inputs(seed: int = 0) -> dict:
    key = jax.random.key(seed)
    ks = iter(jax.random.split(key, 48))
    f32 = jnp.float32

    def nrm(shape, scale):
        return jax.random.normal(next(ks), shape, f32) * scale

    def unif(shape, lo, hi):
        return jax.random.uniform(next(ks), shape, f32, lo, hi)

    def gain(shape):
        return 1.0 + nrm(shape, 0.02)

    n_idx = jnp.arange(S5_STATE, dtype=f32)
    return {
        "x": nrm((BATCH, SEQ, D_MODEL), 1.0),
        "norm_ffa": gain((DEPTH, D_MODEL)),
        "w_ffa_in": nrm((DEPTH, D_MODEL, 2 * D_FF), D_MODEL ** -0.5),
        "w_ffa_out": nrm((DEPTH, D_FF, D_MODEL), D_FF ** -0.5),
        "norm_mix": gain((DEPTH, D_MODEL)),
        "w_in": nrm((DEPTH, D_MODEL, IN_COLS), D_MODEL ** -0.5),
        "rw_mu": unif((DEPTH, RW_COLS), 0.0, 1.0),
        "rw_w0": unif((DEPTH, RW_WIDTH), -6.0, 1.0),
        "rw_w2": nrm((DEPTH, RW_DECAY_RANK, RW_WIDTH), 0.1 * RW_DECAY_RANK ** -0.5),
        "rw_a0": nrm((DEPTH, RW_WIDTH), 0.1),
        "rw_a2": nrm((DEPTH, RW_ICLR_RANK, RW_WIDTH), 0.1 * RW_ICLR_RANK ** -0.5),
        "rw_g2": nrm((DEPTH, RW_GATE_RANK, RW_WIDTH), RW_GATE_RANK ** -0.5),
        "rw_kk": 0.85 + nrm((DEPTH, RW_WIDTH), 0.02),
        "rw_ka": gain((DEPTH, RW_WIDTH)),
        "rw_rk": nrm((DEPTH, RW_HEADS, HEAD_DIM), 0.1),
        "rw_ln_g": gain((DEPTH, RW_WIDTH)),
        "rw_ln_b": nrm((DEPTH, RW_WIDTH), 0.01),
        "gla_wg2": nrm((DEPTH, GLA_GATE_RANK, GLA_QK_WIDTH), GLA_GATE_RANK ** -0.5),
        "gla_bg2": nrm((DEPTH, GLA_QK_WIDTH), 0.1),
        "gla_norm_g": gain((DEPTH, GLA_DV)),
        "fox_bf": unif((DEPTH, FOX_HEADS), 1.0, 4.0),
        "fox_qg": gain((DEPTH, HEAD_DIM)),
        "fox_kg": gain((DEPTH, HEAD_DIM)),
        "s5_lam_re": -0.5 + nrm((DEPTH, S5_GROUPS, S5_STATE), 0.01),
        "s5_lam_im": math.pi * n_idx + nrm((DEPTH, S5_GROUPS, S5_STATE), 0.01),
        "s5_log_dt": unif((DEPTH, S5_GROUPS), math.log(S5_DT_MIN), math.log(S5_DT_MAX)),
        "s5_b_re": nrm((DEPTH, S5_GROUPS, S5_STATE, S5_GROUP), (2.0 * S5_GROUP) ** -0.5),
        "s5_b_im": nrm((DEPTH, S5_GROUPS, S5_STATE, S5_GROUP), (2.0 * S5_GROUP) ** -0.5),
        "s5_c_re": nrm((DEPTH, S5_GROUPS, S5_GROUP, S5_STATE), (2.0 * S5_STATE) ** -0.5),
        "s5_c_im": nrm((DEPTH, S5_GROUPS, S5_GROUP, S5_STATE), (2.0 * S5_STATE) ** -0.5),
        "s5_d": nrm((DEPTH, S5_GROUPS, S5_GROUP), 1.0),
        "s5_w_glu": nrm((DEPTH, S5_WIDTH, S5_WIDTH), S5_WIDTH ** -0.5),
        "s5_b_glu": nrm((DEPTH, S5_WIDTH), 0.01),
        "w_gate": nrm((DEPTH, N_BRANCH, D_MODEL, D_MODEL), D_MODEL ** -0.5),
        "w_branch": nrm((DEPTH, N_BRANCH, BRANCH_WIDTH, D_MODEL), BRANCH_WIDTH ** -0.5),
        "w_o": nrm((DEPTH, D_MODEL, D_MODEL), D_MODEL ** -0.5),
        "norm_ffb": gain((DEPTH, D_MODEL)),
        "w_ffb_in": nrm((DEPTH, D_MODEL, 2 * D_FF), D_MODEL ** -0.5),
        "w_ffb_out": nrm((DEPTH, D_FF, D_MODEL), D_FF ** -0.5),
    }


def reference(x, norm_ffa, w_ffa_in, w_ffa_out, norm_mix, w_in,
              rw_mu, rw_w0, rw_w2, rw_a0, rw_a2, rw_g2, rw_kk, rw_ka, rw_rk, rw_ln_g, rw_ln_b,
              gla_wg2, gla_bg2, gla_norm_g,
              fox_bf, fox_qg, fox_kg,
              s5_lam_re, s5_lam_im, s5_log_dt, s5_b_re, s5_b_im, s5_c_re, s5_c_im, s5_d, s5_w_glu, s5_b_glu,
              w_gate, w_branch, w_o,
              norm_ffb, w_ffb_in, w_ffb_out):
    for i in range(DEPTH):
        x = x + FFN_RES * swiglu(rmsnorm(x, norm_ffa[i]), w_ffa_in[i], w_ffa_out[i])
        h = rmsnorm(x, norm_mix[i])
        c_rw, c_gla, c_fox, c_s5 = _split(h @ w_in[i], [RW_COLS, GLA_COLS, FOX_COLS, S5_COLS])
        branches = (
            rwkv7_mixer(c_rw, rw_mu[i], rw_w0[i], rw_w2[i], rw_a0[i], rw_a2[i], rw_g2[i],
                        rw_kk[i], rw_ka[i], rw_rk[i], rw_ln_g[i], rw_ln_b[i]),
            gla_mixer(c_gla, gla_wg2[i], gla_bg2[i], gla_norm_g[i]),
            fox_mixer(c_fox, fox_bf[i], fox_qg[i], fox_kg[i]),
            s5_mixer(c_s5, s5_lam_re[i], s5_lam_im[i], s5_log_dt[i], s5_b_re[i], s5_b_im[i],
                     s5_c_re[i], s5_c_im[i], s5_d[i], s5_w_glu[i], s5_b_glu[i]),
        )
        merged = jnp.zeros_like(x)
        for j, y in enumerate(branches):
            merged = merged + jax.nn.sigmoid(h @ w_gate[i, j]) * (y @ w_branch[i, j])
        x = x + merged @ w_o[i]
        x = x + FFN_RES * swiglu(rmsnorm(x, norm_ffb[i]), w_ffb_in[i], w_ffb_out[i])
    return x
```

```python
import functools
import math

import jax
import jax.numpy as jnp
from jax import lax
from jax.experimental import pallas as pl
from jax.experimental.pallas import tpu as pltpu

F32 = jnp.float32
BF16 = jnp.bfloat16

D_MODEL = 2048
D_FF = 5632
FFN_RES = 0.5
NORM_EPS = 1e-6
HEAD_DIM = 64
CHUNK = 64

RW_HEADS = 8
RW_WIDTH = 512
RW_COLS = 1792
RW_GN_EPS = 64e-5

GLA_HEADS = 4
GLA_DK = 64
GLA_DV = 128
GLA_GATE_RANK = 16
GLA_GATE_NORM = 16.0
GLA_PACKED = 1664

FOX_HEADS = 8
FOX_PACKED = 1664
FOX_AUG = 128

S5_GROUPS = 32
S5_GROUP = 16
S5_STATE = 64
S5_BLOCK = 16
S5_ROW = S5_BLOCK * S5_GROUP

VMEM_LIMIT = 56 * 1024 * 1024


def _cparams(sem):
    return pltpu.CompilerParams(dimension_semantics=sem, vmem_limit_bytes=VMEM_LIMIT)


def _bdot(a, b):
    return jnp.dot(a.astype(BF16), b.astype(BF16), preferred_element_type=F32)


def _bdot_nt(a, b):
    return lax.dot_general(a.astype(BF16), b.astype(BF16), (((1,), (1,)), ((), ())),
                           preferred_element_type=F32)


def _bdot_tn(a, b):
    return lax.dot_general(a.astype(BF16), b.astype(BF16), (((0,), (0,)), ((), ())),
                           preferred_element_type=F32)


def _split3(x):
    h1 = x.astype(BF16)
    r1 = x - h1.astype(F32)
    h2 = r1.astype(BF16)
    h3 = (r1 - h2.astype(F32)).astype(BF16)
    return h1, h2, h3


def _exact_left_dot(m_bf16, x):
    h1, h2, h3 = _split3(x)
    d = lambda h: jnp.dot(m_bf16, h, preferred_element_type=F32)
    return d(h1) + d(h2) + d(h3)


def _exact_right_dot(x, m_bf16):
    h1, h2, h3 = _split3(x)
    d = lambda h: jnp.dot(h, m_bf16, preferred_element_type=F32)
    return d(h1) + d(h2) + d(h3)


def _dot3(a, b):
    ah = a.astype(BF16)
    al = (a - ah.astype(F32)).astype(BF16)
    bh = b.astype(BF16)
    bl = (b - bh.astype(F32)).astype(BF16)
    d = lambda p, q: jnp.dot(p, q, preferred_element_type=F32)
    return d(ah, bh) + d(ah, bl) + d(al, bh)


def _tri_masks(n):
    row = lax.broadcasted_iota(jnp.int32, (n, n), 0)
    col = lax.broadcasted_iota(jnp.int32, (n, n), 1)
    return row, col


def _sigmoid(x):
    return 1.0 / (1.0 + jnp.exp(-x))


def _softplus(x):
    return jnp.maximum(x, 0.0) + jnp.log(1.0 + jnp.exp(-jnp.abs(x)))


def _log_sigmoid(x):
    return -_softplus(-x)


def _silu(x):
    return x * _sigmoid(x)


def _unit_lower_inverse(a, row, col):
    eye = jnp.where(row == col, 1.0, 0.0).astype(F32)
    ad = jnp.where((row // 8) == (col // 8), a, 0.0)
    x = eye + ad
    a2 = _dot3(ad, ad)
    x = x + _dot3(a2, x)
    a4 = _dot3(a2, a2)
    x = x + _dot3(a4, x)
    s = 8
    while s < CHUNK:
        e = jnp.where(((row // (2 * s)) == (col // (2 * s))) & ((row // s) != (col // s)), a, 0.0)
        x = x + _dot3(x, _dot3(e, x))
        s *= 2
    return x


def _rwkv7_kernel(c_ref, mu_ref, vec_ref, w2_ref, a2_ref, g2_ref, ones_ref, o_ref,
                  prev_sc, state_sc, y_sc, *, tl):
    li = pl.program_id(1)

    @pl.when(li == 0)
    def _():
        prev_sc[...] = jnp.zeros_like(prev_sc)
        state_sc[...] = jnp.zeros_like(state_sc)

    c = c_ref[0]
    rows = lax.broadcasted_iota(jnp.int32, (tl, 1), 0)
    c_prev = jnp.where(rows == 0, prev_sc[...], pltpu.roll(c, 1, axis=0))
    prev_sc[...] = c[tl - 1:tl, :]
    c = c + (c_prev - c) * mu_ref[...]

    w0, a0, k_k, k_a = vec_ref[0:1, :], vec_ref[1:2, :], vec_ref[2:3, :], vec_ref[3:4, :]
    r_k, ln_g, ln_b = vec_ref[4:5, :], vec_ref[5:6, :], vec_ref[6:7, :]
    ones_bd = ones_ref[...]

    r = c[:, 0:512]
    k = c[:, 512:1024]
    v = c[:, 1024:1536]
    xw = c[:, 1536:1600]
    xa = c[:, 1600:1664]
    xg = c[:, 1664:1792]

    w_log = -_softplus(-(w0 + _bdot(jnp.tanh(xw), w2_ref[...]))) - 0.5
    logw = -jnp.exp(w_log)
    a = _sigmoid(a0 + _bdot(xa, a2_ref[...]))
    g = _bdot(_sigmoid(xg), g2_ref[...])

    kk = k * k_k
    kk = kk / jnp.maximum(jnp.sqrt(_exact_right_dot(kk * kk, ones_bd)), 1e-12)
    k = k * (1.0 + (a - 1.0) * k_a)
    bonus = _exact_right_dot(r * k * r_k, ones_bd)
    av = -kk
    bv = kk * a

    row, col = _tri_masks(CHUNK)
    tril_incl = jnp.where(row >= col, 1.0, 0.0).astype(BF16)
    strict = row > col
    incl = row >= col

    for j in range(tl // CHUNK):
        sl = slice(j * CHUNK, (j + 1) * CHUNK)
        lw = logw[sl]
        cum = _exact_left_dot(tril_incl, lw)
        cum_last = cum[CHUNK - 1:CHUNK, :]
        e_neg = jnp.exp(-cum)
        a_t = av[sl] * jnp.exp(cum - lw)
        r_t = r[sl] * jnp.exp(cum)
        b_t = bv[sl] * e_neg
        k_t = k[sl] * e_neg
        e_end = jnp.exp(cum_last - cum)
        b_e = bv[sl] * e_end
        k_e = k[sl] * e_end
        p_c = jnp.exp(cum_last)
        vj = v[sl]
        for h in range(RW_HEADS):
            hs = slice(h * HEAD_DIM, (h + 1) * HEAD_DIM)
            lhs = jnp.concatenate([a_t[:, hs], r_t[:, hs]], axis=0)
            rhs = jnp.concatenate([b_t[:, hs], k_t[:, hs]], axis=0)
            aall = _bdot_nt(lhs, rhs)
            a_ab = jnp.where(strict, aall[:CHUNK, :CHUNK], 0.0)
            a_ak = jnp.where(strict, aall[:CHUNK, CHUNK:], 0.0)
            a_rb = jnp.where(incl, aall[CHUNK:, :CHUNK], 0.0)
            a_rk = jnp.where(incl, aall[CHUNK:, CHUNK:], 0.0)
            tinv = _unit_lower_inverse(a_ab, row, col)
            s0 = state_sc[h]
            vh = vj[:, hs]
            proj = _bdot_nt(lhs, s0)
            u = _dot3(tinv, proj[:CHUNK] + _bdot(a_ak, vh))
            y = proj[CHUNK:] + _bdot(a_rb, u) + _bdot(a_rk, vh)
            uv = jnp.concatenate([u, vh], axis=0)
            be = jnp.concatenate([b_e[:, hs], k_e[:, hs]], axis=0)
            state_sc[h] = s0 * p_c[:, hs] + _bdot_tn(uv, be)
            y_sc[sl, hs] = y

    y = y_sc[...]
    mean = _exact_right_dot(y, ones_bd) * (1.0 / HEAD_DIM)
    yc = y - mean
    var = _exact_right_dot(yc * yc, ones_bd) * (1.0 / HEAD_DIM)
    y = yc * lax.rsqrt(var + RW_GN_EPS) * ln_g + ln_b
    y = y + bonus * v
    o_ref[0] = (y * g).astype(o_ref.dtype)


def _rwkv7(c_rw, mu, vec, w2, a2, g2, ones64, *, tl=CHUNK):
    b, l, _ = c_rw.shape
    full = lambda shape: pl.BlockSpec(shape, lambda bi, li: (0,) * len(shape))
    return pl.pallas_call(
        functools.partial(_rwkv7_kernel, tl=tl),
        grid=(b, l // tl),
        in_specs=[
            pl.BlockSpec((1, tl, RW_COLS), lambda bi, li: (bi, li, 0)),
            full((1, RW_COLS)), full((8, RW_WIDTH)), full((64, RW_WIDTH)), full((64, RW_WIDTH)),
            full((128, RW_WIDTH)), full((RW_WIDTH, RW_WIDTH)),
        ],
        out_specs=pl.BlockSpec((1, tl, RW_WIDTH), lambda bi, li: (bi, li, 0)),
        out_shape=jax.ShapeDtypeStruct((b, l, RW_WIDTH), BF16),
        scratch_shapes=[
            pltpu.VMEM((1, RW_COLS), F32),
            pltpu.VMEM((RW_HEADS, HEAD_DIM, HEAD_DIM), F32),
            pltpu.VMEM((tl, RW_WIDTH), F32),
        ],
        compiler_params=_cparams(("parallel", "arbitrary")),
        name="rwkv7_mixer",
    )(c_rw, mu, vec, w2, a2, g2, ones64)


def _block_ones(width, block):
    i = jnp.arange(width) // block
    return (i[:, None] == i[None, :]).astype(BF16)


def _rwkv7_params(mu, w0, w2, a0, a2, g2, k_k, k_a, r_k, ln_g, ln_b):
    vec = jnp.stack([w0, a0, k_k, k_a, r_k.reshape(-1), ln_g, ln_b, jnp.zeros_like(w0)]).astype(F32)
    return (mu.reshape(1, -1).astype(F32), vec, w2.astype(BF16), a2.astype(BF16), g2.astype(BF16),
            _block_ones(RW_WIDTH, HEAD_DIM))


def _gla_kernel(c_ref, wg_ref, vec_ref, o_ref, state_sc, o_sc, *, tl):
    li = pl.program_id(1)

    @pl.when(li == 0)
    def _():
        state_sc[...] = jnp.zeros_like(state_sc)

    c = c_ref[0]
    q = c[:, 0:256] * (GLA_DK ** -0.5)
    k = c[:, 256:512]
    v = c[:, 512:1024]
    og = c[:, 1024:1536]
    glr = c[:, 1536:1664]
    b_g2 = vec_ref[0:1, 0:256]
    norm_g = vec_ref[1:2, :]
    log_a = _log_sigmoid(_bdot(glr, wg_ref[...]) + b_g2) * (1.0 / GLA_GATE_NORM)

    row, col = _tri_masks(CHUNK)
    tril_incl = jnp.where(row >= col, 1.0, 0.0).astype(BF16)
    incl = row >= col

    for j in range(tl // CHUNK):
        sl = slice(j * CHUNK, (j + 1) * CHUNK)
        cum = _exact_left_dot(tril_incl, log_a[sl])
        cum_last = cum[CHUNK - 1:CHUNK, :]
        q_dec = q[sl] * jnp.exp(cum)
        k_inv = k[sl] * jnp.exp(-cum)
        k_end = k[sl] * jnp.exp(cum_last - cum)
        dec = jnp.exp(cum_last)
        for h in range(GLA_HEADS):
            ks = slice(h * GLA_DK, (h + 1) * GLA_DK)
            vs = slice(h * GLA_DV, (h + 1) * GLA_DV)
            att = jnp.where(incl, _bdot_nt(q_dec[:, ks], k_inv[:, ks]), 0.0)
            vh = v[sl, vs]
            s0 = state_sc[h]
            o_sc[sl, vs] = _bdot(att, vh) + _bdot_nt(q_dec[:, ks], s0)
            state_sc[h] = s0 * dec[:, ks] + _bdot_tn(vh, k_end[:, ks])

    o = o_sc[...]
    outs = []
    for h in range(GLA_HEADS):
        oh = o[:, h * GLA_DV:(h + 1) * GLA_DV]
        ms = jnp.mean(oh * oh, axis=-1, keepdims=True)
        outs.append(oh * lax.rsqrt(ms + NORM_EPS) * norm_g[:, 0:GLA_DV])
    o = jnp.concatenate(outs, axis=-1)
    o_ref[0] = (o * _silu(og)).astype(o_ref.dtype)


def _gla(c_gla, wg, vec, *, tl=CHUNK):
    b, l, _ = c_gla.shape
    full = lambda shape: pl.BlockSpec(shape, lambda bi, li: (0,) * len(shape))
    return pl.pallas_call(
        functools.partial(_gla_kernel, tl=tl),
        grid=(b, l // tl),
        in_specs=[
            pl.BlockSpec((1, tl, GLA_PACKED), lambda bi, li: (bi, li, 0)),
            full((128, 256)), full((8, 512)),
        ],
        out_specs=pl.BlockSpec((1, tl, 512), lambda bi, li: (bi, li, 0)),
        out_shape=jax.ShapeDtypeStruct((b, l, 512), BF16),
        scratch_shapes=[
            pltpu.VMEM((GLA_HEADS, GLA_DV, GLA_DK), F32),
            pltpu.VMEM((tl, 512), F32),
        ],
        compiler_params=_cparams(("parallel", "arbitrary")),
        name="gla_mixer",
    )(c_gla, wg, vec)


def _gla_params(w_g2, b_g2, norm_g):
    wg = jnp.zeros((128, 256), F32).at[:GLA_GATE_RANK].set(w_g2).astype(BF16)
    vec = jnp.zeros((8, 512), F32).at[0, :256].set(b_g2).at[1, :GLA_DV].set(norm_g)
    return wg, vec


NEG_BIG = -1e30


def _fox_prep_kernel(c_ref, vec_ref, bf_ref, ones_ref, q_ref, k_ref, v_ref, carry_sc, *, tp):
    li = pl.program_id(1)

    @pl.when(li == 0)
    def _():
        carry_sc[...] = jnp.zeros_like(carry_sc)

    c = c_ref[0]
    ones_bd = ones_ref[...]
    q = c[:, 0:512]
    k = c[:, 512:1024]
    v_ref[0] = c[:, 1024:1536].astype(v_ref.dtype)
    q_ms = _exact_right_dot(q * q, ones_bd) * (1.0 / HEAD_DIM)
    k_ms = _exact_right_dot(k * k, ones_bd) * (1.0 / HEAD_DIM)
    qn = q * lax.rsqrt(q_ms + NORM_EPS) * vec_ref[0:1, :] * (HEAD_DIM ** -0.5)
    kn = k * lax.rsqrt(k_ms + NORM_EPS) * vec_ref[1:2, :]

    log_f = _log_sigmoid(c[:, 1536:1664] + bf_ref[0:1, :])
    row, col = _tri_masks(tp)
    tril_incl = jnp.where(row >= col, 1.0, 0.0).astype(BF16)
    cum = _exact_left_dot(tril_incl, log_f) + carry_sc[...]
    carry_sc[...] = cum[tp - 1:tp, :]

    lane = lax.broadcasted_iota(jnp.int32, (tp, HEAD_DIM), 1)
    for h in range(FOX_HEADS):
        f = jnp.broadcast_to(cum[:, h:h + 1], (tp, HEAD_DIM))
        f1 = f.astype(BF16).astype(F32)
        r1 = f - f1
        f2 = r1.astype(BF16).astype(F32)
        f3 = r1 - f2
        fq = jnp.where(lane == 0, f1, jnp.where(lane == 1, f2, jnp.where(lane == 2, f3, 0.0)))
        aug_q = jnp.where(lane < 3, fq, jnp.where(lane < 6, 1.0, 0.0))
        fk = jnp.where(lane == 3, f1, jnp.where(lane == 4, f2, jnp.where(lane == 5, f3, 0.0)))
        aug_k = jnp.where(lane < 3, 1.0, -fk)
        hs = slice(h * HEAD_DIM, (h + 1) * HEAD_DIM)
        q_ref[0, :, h * FOX_AUG:(h + 1) * FOX_AUG] = jnp.concatenate([qn[:, hs], aug_q], axis=-1).astype(q_ref.dtype)
        k_ref[0, :, h * FOX_AUG:(h + 1) * FOX_AUG] = jnp.concatenate([kn[:, hs], aug_k], axis=-1).astype(k_ref.dtype)


def _fox_attn_kernel(qi_ref, ki_ref, q_ref, k_ref, v_ref, o_ref, m_sc, l_sc, acc_sc, *, tq, tk):
    s = pl.program_id(2)
    qi = qi_ref[s]
    ki = ki_ref[s]
    last_k = ((qi + 1) * tq - 1) // tk

    @pl.when(ki == 0)
    def _():
        m_sc[...] = jnp.full_like(m_sc, NEG_BIG)
        l_sc[...] = jnp.zeros_like(l_sc)
        acc_sc[...] = jnp.zeros_like(acc_sc)

    def step(masked):
        for hh in range(2):
            qs = slice(hh * FOX_AUG, (hh + 1) * FOX_AUG)
            logits = lax.dot_general(q_ref[0, :, qs], k_ref[0, :, qs], (((1,), (1,)), ((), ())),
                                     preferred_element_type=F32)
            if masked:
                rpos = qi * tq + lax.broadcasted_iota(jnp.int32, (tq, tk), 0)
                cpos = ki * tk + lax.broadcasted_iota(jnp.int32, (tq, tk), 1)
                logits = jnp.where(cpos > rpos, NEG_BIG, logits)
            m_old = m_sc[hh]
            m_new = jnp.maximum(m_old, jnp.max(logits, axis=-1, keepdims=True))
            alpha = jnp.exp(m_old - m_new)
            p = jnp.exp(logits - m_new)
            l_sc[hh] = alpha * l_sc[hh] + jnp.sum(p, axis=-1, keepdims=True)
            pv = jnp.dot(p.astype(BF16), v_ref[0, :, hh * HEAD_DIM:(hh + 1) * HEAD_DIM],
                         preferred_element_type=F32)
            acc_sc[hh] = alpha * acc_sc[hh] + pv
            m_sc[hh] = m_new

    needs_mask = (ki + 1) * tk - 1 > qi * tq

    @pl.when(needs_mask)
    def _():
        step(True)

    @pl.when(jnp.logical_not(needs_mask))
    def _():
        step(False)

    @pl.when(ki == last_k)
    def _():
        o_ref[0] = jnp.concatenate([acc_sc[0] / l_sc[0], acc_sc[1] / l_sc[1]], axis=-1).astype(o_ref.dtype)


def _fox(c_fox, vec, bf, ones64, *, tp=256, tq=512, tk=256):
    b, l, _ = c_fox.shape
    tp, tq, tk = min(tp, l), min(tq, l), min(tk, l)
    full = lambda shape: pl.BlockSpec(shape, lambda bi, li: (0,) * len(shape))
    q_aug, k_aug, v_bf = pl.pallas_call(
        functools.partial(_fox_prep_kernel, tp=tp),
        grid=(b, l // tp),
        in_specs=[
            pl.BlockSpec((1, tp, FOX_PACKED), lambda bi, li: (bi, li, 0)),
            full((8, 512)), full((8, 128)), full((512, 512)),
        ],
        out_specs=[
            pl.BlockSpec((1, tp, FOX_HEADS * FOX_AUG), lambda bi, li: (bi, li, 0)),
            pl.BlockSpec((1, tp, FOX_HEADS * FOX_AUG), lambda bi, li: (bi, li, 0)),
            pl.BlockSpec((1, tp, 512), lambda bi, li: (bi, li, 0)),
        ],
        out_shape=[
            jax.ShapeDtypeStruct((b, l, FOX_HEADS * FOX_AUG), BF16),
            jax.ShapeDtypeStruct((b, l, FOX_HEADS * FOX_AUG), BF16),
            jax.ShapeDtypeStruct((b, l, 512), BF16),
        ],
        scratch_shapes=[pltpu.VMEM((1, 128), F32)],
        compiler_params=_cparams(("parallel", "arbitrary")),
        name="fox_prep",
    )(c_fox, vec, bf, ones64)

    pairs = [(qi, ki) for qi in range(l // tq) for ki in range(((qi + 1) * tq - 1) // tk + 1)]
    qi_tab = jnp.asarray([p[0] for p in pairs], jnp.int32)
    ki_tab = jnp.asarray([p[1] for p in pairs], jnp.int32)
    grid_spec = pltpu.PrefetchScalarGridSpec(
        num_scalar_prefetch=2,
        grid=(b, FOX_HEADS // 2, len(pairs)),
        in_specs=[
            pl.BlockSpec((1, tq, 2 * FOX_AUG), lambda bi, p, s, qt, kt: (bi, qt[s], p)),
            pl.BlockSpec((1, tk, 2 * FOX_AUG), lambda bi, p, s, qt, kt: (bi, kt[s], p)),
            pl.BlockSpec((1, tk, 2 * HEAD_DIM), lambda bi, p, s, qt, kt: (bi, kt[s], p)),
        ],
        out_specs=pl.BlockSpec((1, tq, 2 * HEAD_DIM), lambda bi, p, s, qt, kt: (bi, qt[s], p)),
        scratch_shapes=[
            pltpu.VMEM((2, tq, 1), F32),
            pltpu.VMEM((2, tq, 1), F32),
            pltpu.VMEM((2, tq, HEAD_DIM), F32),
        ],
    )
    return pl.pallas_call(
        functools.partial(_fox_attn_kernel, tq=tq, tk=tk),
        grid_spec=grid_spec,
        out_shape=jax.ShapeDtypeStruct((b, l, 512), BF16),
        compiler_params=_cparams(("parallel", "parallel", "arbitrary")),
        name="fox_attention",
    )(qi_tab, ki_tab, q_aug, k_aug, v_bf)


def _fox_params(b_f, q_g, k_g):
    vec = jnp.zeros((8, 512), F32).at[0].set(jnp.tile(q_g, FOX_HEADS)).at[1].set(jnp.tile(k_g, FOX_HEADS))
    bf = jnp.zeros((8, 128), F32).at[0, :FOX_HEADS].set(b_f)
    return vec, bf, _block_ones(512, HEAD_DIM)


def _gelu_tanh(x):
    return 0.5 * x * (1.0 + jnp.tanh(math.sqrt(2.0 / math.pi) * (x + 0.044715 * (x * x * x))))


def _s5_core_kernel(u_ref, mt_ref, bend_ref, cpow_ref, apow_ref, d_ref, y_ref, *, nb, levels):
    u = u_ref[0]
    rows = u.shape[0]
    ub = u.astype(BF16)
    y = jnp.dot(ub, mt_ref[0], preferred_element_type=F32)
    x = jnp.dot(ub, bend_ref[0], preferred_element_type=F32)
    blk = lax.broadcasted_iota(jnp.int32, (rows, 1), 0) % nb
    lane = lax.broadcasted_iota(jnp.int32, (1, 2 * S5_STATE), 1)
    for lv in range(levels):
        sh = 1 << lv
        ap = apow_ref[0, lv:lv + 1, :]
        ap_sw = pltpu.roll(ap, S5_STATE, axis=1)
        c1 = jnp.where(lane < S5_STATE, ap, ap_sw)
        c2 = jnp.where(lane < S5_STATE, -ap_sw, ap)
        xs = jnp.where(blk >= sh, pltpu.roll(x, sh, axis=0), 0.0)
        x = x + c1 * xs + c2 * pltpu.roll(xs, S5_STATE, axis=1)
    x_in = jnp.where(blk >= 1, pltpu.roll(x, 1, axis=0), 0.0)
    y = y + jnp.dot(x_in.astype(BF16), cpow_ref[0], preferred_element_type=F32)
    y_ref[0] = y + u * d_ref[0]


def _s5_glu_kernel(y_ref, w_ref, b_ref, o_ref):
    z = _gelu_tanh(y_ref[...])
    o_ref[...] = (z * _sigmoid(_bdot(z, w_ref[...]) + b_ref[0:1, :])).astype(o_ref.dtype)


def _s5(u, mt, bend, cpow, apow, d_row, w_glu, b_glu):
    b, l, _ = u.shape
    nb = l // S5_BLOCK
    rows = b * nb
    levels = max(1, (nb - 1).bit_length())
    ut = u.reshape(b, nb, S5_BLOCK, S5_GROUPS, S5_GROUP).transpose(3, 0, 1, 2, 4).reshape(S5_GROUPS, rows, S5_ROW)
    per_g = lambda shape: pl.BlockSpec((1,) + shape, lambda g: (g, 0, 0))
    yt = pl.pallas_call(
        functools.partial(_s5_core_kernel, nb=nb, levels=levels),
        grid=(S5_GROUPS,),
        in_specs=[per_g((rows, S5_ROW)), per_g((S5_ROW, S5_ROW)), per_g((S5_ROW, 2 * S5_STATE)),
                  per_g((2 * S5_STATE, S5_ROW)), per_g((8, 2 * S5_STATE)), per_g((1, S5_ROW))],
        out_specs=per_g((rows, S5_ROW)),
        out_shape=jax.ShapeDtypeStruct((S5_GROUPS, rows, S5_ROW), F32),
        compiler_params=_cparams(("parallel",)),
        name="s5_core",
    )(ut, mt, bend, cpow, apow, d_row)
    y = yt.reshape(S5_GROUPS, b, nb, S5_BLOCK, S5_GROUP).transpose(1, 2, 3, 0, 4).reshape(b * l, 512)
    tm = min(512, b * l)
    out = pl.pallas_call(
        _s5_glu_kernel,
        grid=(b * l // tm,),
        in_specs=[pl.BlockSpec((tm, 512), lambda i: (i, 0)),
                  pl.BlockSpec((512, 512), lambda i: (0, 0)),
                  pl.BlockSpec((8, 512), lambda i: (0, 0))],
        out_specs=pl.BlockSpec((tm, 512), lambda i: (i, 0)),
        out_shape=jax.ShapeDtypeStruct((b * l, 512), BF16),
        compiler_params=_cparams(("parallel",)),
        name="s5_glu",
    )(y, w_glu, b_glu)
    return out.reshape(b, l, 512)


def _s5_params(lam_re, lam_im, log_dt, b_re, b_im, c_re, c_im, d_skip, w_glu, b_glu):
    t = S5_BLOCK
    lr, li = lam_re.astype(F32), lam_im.astype(F32)
    dt = jnp.exp(log_dt.astype(F32))[:, None]
    mag = jnp.exp(lr * dt)
    ar, ai = mag * jnp.cos(li * dt), mag * jnp.sin(li * dt)
    den = lr * lr + li * li
    zr = ((ar - 1.0) * lr + ai * li) / den
    zi = (ai * lr - (ar - 1.0) * li) / den
    br, bi = b_re.astype(F32), b_im.astype(F32)
    bbr = zr[..., None] * br - zi[..., None] * bi
    bbi = zr[..., None] * bi + zi[..., None] * br
    pr, pi = [jnp.ones_like(ar)], [jnp.zeros_like(ar)]
    for _ in range(t):
        pr, pi = pr + [pr[-1] * ar - pi[-1] * ai], pi + [pr[-1] * ai + pi[-1] * ar]
    pr, pi = jnp.stack(pr), jnp.stack(pi)
    cr, ci = c_re.astype(F32), c_im.astype(F32)
    vr = pr[..., None] * bbr[None] - pi[..., None] * bbi[None]
    vi = pr[..., None] * bbi[None] + pi[..., None] * bbr[None]
    kern = jnp.einsum('ghp,tgpj->tghj', cr, vr[:t]) - jnp.einsum('ghp,tgpj->tghj', ci, vi[:t])
    sidx = jnp.arange(t)
    tau = sidx[None, :] - sidx[:, None]
    blocks = jnp.where((tau >= 0)[..., None, None, None], kern[jnp.clip(tau, 0, t - 1)], 0.0)
    mt = blocks.transpose(2, 0, 4, 1, 3).reshape(S5_GROUPS, S5_ROW, S5_ROW)
    vend_r = vr[t - 1 - sidx]
    vend_i = vi[t - 1 - sidx]
    bend = jnp.concatenate([vend_r.transpose(1, 0, 3, 2).reshape(S5_GROUPS, S5_ROW, S5_STATE),
                            vend_i.transpose(1, 0, 3, 2).reshape(S5_GROUPS, S5_ROW, S5_STATE)], axis=-1)
    p1r, p1i = pr[1:], pi[1:]
    cre = cr[None] * p1r[:, :, None, :] - ci[None] * p1i[:, :, None, :]
    cim = -cr[None] * p1i[:, :, None, :] - ci[None] * p1r[:, :, None, :]
    cpow = jnp.concatenate([cre.transpose(1, 3, 0, 2).reshape(S5_GROUPS, S5_STATE, S5_ROW),
                            cim.transpose(1, 3, 0, 2).reshape(S5_GROUPS, S5_STATE, S5_ROW)], axis=1)
    qr, qi, rows = pr[t], pi[t], []
    for _ in range(8):
        rows.append(jnp.concatenate([qr, qi], axis=-1))
        qr, qi = qr * qr - qi * qi, 2.0 * qr * qi
    apow = jnp.stack(rows, axis=1)
    d_row = jnp.tile(d_skip.astype(F32), (1, t)).reshape(S5_GROUPS, 1, S5_ROW)
    b_pad = jnp.zeros((8, 512), F32).at[0].set(b_glu.astype(F32))
    return (mt.astype(BF16), bend.astype(BF16), cpow.astype(BF16), apow, d_row,
            w_glu.astype(BF16), b_pad)


def _rmsnorm_bf16(x, gain):
    ms = jnp.mean(x * x, axis=-1, keepdims=True)
    return (x * lax.rsqrt(ms + NORM_EPS) * gain).astype(BF16)


def _ffn_kernel(x_ref, g_ref, wg_ref, wu_ref, wo_ref, o_ref, h_sc, acc_sc):
    f = pl.program_id(1)

    @pl.when(f == 0)
    def _():
        h_sc[...] = _rmsnorm_bf16(x_ref[...], g_ref[0:1, :])
        acc_sc[...] = jnp.zeros_like(acc_sc)

    h = h_sc[...]
    gate = jnp.dot(h, wg_ref[...], preferred_element_type=F32)
    up = jnp.dot(h, wu_ref[...], preferred_element_type=F32)
    act = (_silu(gate) * up).astype(BF16)
    acc_sc[...] += jnp.dot(act, wo_ref[...], preferred_element_type=F32)

    @pl.when(f == pl.num_programs(1) - 1)
    def _():
        o_ref[...] = x_ref[...] + FFN_RES * acc_sc[...]


def _ffn(x, gain, w_in, w_out, *, tm=512, tf=512):
    m, d = x.shape
    d_ff = w_out.shape[0]
    tm = min(tm, m)
    nf = d_ff // tf
    return pl.pallas_call(
        _ffn_kernel,
        grid=(m // tm, nf),
        in_specs=[
            pl.BlockSpec((tm, d), lambda i, f: (i, 0)),
            pl.BlockSpec((8, d), lambda i, f: (0, 0)),
            pl.BlockSpec((d, tf), lambda i, f: (0, f)),
            pl.BlockSpec((d, tf), lambda i, f: (0, nf + f)),
            pl.BlockSpec((tf, d), lambda i, f: (f, 0)),
        ],
        out_specs=pl.BlockSpec((tm, d), lambda i, f: (i, 0)),
        out_shape=jax.ShapeDtypeStruct((m, d), F32),
        scratch_shapes=[pltpu.VMEM((tm, d), BF16), pltpu.VMEM((tm, d), F32)],
        compiler_params=_cparams(("parallel", "arbitrary")),
        name="ffn",
    )(x, gain, w_in, w_in, w_out)


def _proj_kernel(x_ref, g_ref, w_rw_ref, w_gla_ref, w_fox_ref, w_s5_ref, rw_ref, gla_ref, fox_ref, s5_ref):
    h = _rmsnorm_bf16(x_ref[...], g_ref[0:1, :])
    rw_ref[...] = jnp.dot(h, w_rw_ref[...], preferred_element_type=F32)
    gla_ref[...] = jnp.dot(h, w_gla_ref[...], preferred_element_type=F32)
    fox_ref[...] = jnp.dot(h, w_fox_ref[...], preferred_element_type=F32)
    s5_ref[...] = jnp.dot(h, w_s5_ref[...], preferred_element_type=F32)


def _proj(x, gain, w_rw, w_gla, w_fox, w_s5, *, tm=256):
    m, d = x.shape
    tm = min(tm, m)
    widths = (RW_COLS, GLA_PACKED, FOX_PACKED, 512)
    resident = lambda n: pl.BlockSpec((d, n), lambda i: (0, 0), pipeline_mode=pl.Buffered(1))
    return pl.pallas_call(
        _proj_kernel,
        grid=(m // tm,),
        in_specs=[pl.BlockSpec((tm, d), lambda i: (i, 0)), pl.BlockSpec((8, d), lambda i: (0, 0))]
                 + [resident(n) for n in widths],
        out_specs=[pl.BlockSpec((tm, n), lambda i: (i, 0)) for n in widths],
        out_shape=[jax.ShapeDtypeStruct((m, n), F32) for n in widths],
        compiler_params=_cparams(("parallel",)),
        name="mixer_in_proj",
    )(x, gain, w_rw, w_gla, w_fox, w_s5)


def _merge_kernel(x_ref, g_ref, y0_ref, y1_ref, y2_ref, y3_ref, wg_ref, wb_ref, wo_ref, o_ref, h_sc, acc_sc):
    n = pl.program_id(1)

    @pl.when(n == 0)
    def _():
        h_sc[...] = _rmsnorm_bf16(x_ref[...], g_ref[0:1, :])
        acc_sc[...] = jnp.zeros_like(acc_sc)

    h = h_sc[...]
    merged = None
    for j, y_ref in enumerate((y0_ref, y1_ref, y2_ref, y3_ref)):
        gate = _sigmoid(jnp.dot(h, wg_ref[j], preferred_element_type=F32))
        term = gate * jnp.dot(y_ref[...], wb_ref[j], preferred_element_type=F32)
        merged = term if merged is None else merged + term
    acc_sc[...] += jnp.dot(merged.astype(BF16), wo_ref[...], preferred_element_type=F32)

    @pl.when(n == pl.num_programs(1) - 1)
    def _():
        o_ref[...] = x_ref[...] + acc_sc[...]


def _merge(x, gain, ys, w_gate, w_branch, w_o, *, tm=512, tn=256):
    m, d = x.shape
    tm = min(tm, m)
    bw = ys[0].shape[1]
    return pl.pallas_call(
        _merge_kernel,
        grid=(m // tm, d // tn),
        in_specs=[pl.BlockSpec((tm, d), lambda i, n: (i, 0)), pl.BlockSpec((8, d), lambda i, n: (0, 0))]
                 + [pl.BlockSpec((tm, bw), lambda i, n: (i, 0)) for _ in ys]
                 + [pl.BlockSpec((4, d, tn), lambda i, n: (0, 0, n)),
                    pl.BlockSpec((4, bw, tn), lambda i, n: (0, 0, n)),
                    pl.BlockSpec((tn, d), lambda i, n: (n, 0))],
        out_specs=pl.BlockSpec((tm, d), lambda i, n: (i, 0)),
        out_shape=jax.ShapeDtypeStruct((m, d), F32),
        scratch_shapes=[pltpu.VMEM((tm, d), BF16), pltpu.VMEM((tm, d), F32)],
        compiler_params=_cparams(("parallel", "arbitrary")),
        name="merge",
    )(x, gain, *ys, w_gate, w_branch, w_o)


def _gain_rows(g):
    return jnp.broadcast_to(g.astype(F32)[None, :], (8, g.shape[0]))


def _pack_w_in(w):
    d = w.shape[0]
    o = 0
    w_rw = w[:, o:o + RW_COLS]
    o += RW_COLS
    q, k, v, glr, og = (w[:, o:o + 256], w[:, o + 256:o + 512], w[:, o + 512:o + 1024],
                        w[:, o + 1024:o + 1040], w[:, o + 1040:o + 1552])
    w_gla = jnp.concatenate([q, k, v, og, glr, jnp.zeros((d, 128 - GLA_GATE_RANK), w.dtype)], axis=1)
    o += 1552
    w_fox = jnp.concatenate([w[:, o:o + 1544], jnp.zeros((d, 128 - FOX_HEADS), w.dtype)], axis=1)
    o += 1544
    w_s5 = w[:, o:o + 512]
    return [t.astype(BF16) for t in (w_rw, w_gla, w_fox, w_s5)]


def kernel(x, norm_ffa, w_ffa_in, w_ffa_out, norm_mix, w_in, rw_mu, rw_w0, rw_w2, rw_a0, rw_a2, rw_g2, rw_kk, rw_ka, rw_rk, rw_ln_g, rw_ln_b, gla_wg2, gla_bg2, gla_norm_g, fox_bf, fox_qg, fox_kg, s5_lam_re, s5_lam_im, s5_log_dt, s5_b_re, s5_b_im, s5_c_re, s5_c_im, s5_d, s5_w_glu, s5_b_glu, w_gate, w_branch, w_o, norm_ffb, w_ffb_in, w_ffb_out):
    b, l, d = x.shape
    m = b * l
    x = x.reshape(m, d).astype(F32)
    for i in range(norm_ffa.shape[0]):
        x = _ffn(x, _gain_rows(norm_ffa[i]), w_ffa_in[i].astype(BF16), w_ffa_out[i].astype(BF16))
        gain_mix = _gain_rows(norm_mix[i])
        c_rw, c_gla, c_fox, c_s5 = _proj(x, gain_mix, *_pack_w_in(w_in[i]))
        y_rw = _rwkv7(c_rw.reshape(b, l, -1),
                      *_rwkv7_params(rw_mu[i], rw_w0[i], rw_w2[i], rw_a0[i], rw_a2[i], rw_g2[i],
                                     rw_kk[i], rw_ka[i], rw_rk[i], rw_ln_g[i], rw_ln_b[i]))
        y_gla = _gla(c_gla.reshape(b, l, -1), *_gla_params(gla_wg2[i], gla_bg2[i], gla_norm_g[i]))
        y_fox = _fox(c_fox.reshape(b, l, -1), *_fox_params(fox_bf[i], fox_qg[i], fox_kg[i]))
        y_s5 = _s5(c_s5.reshape(b, l, -1),
                   *_s5_params(s5_lam_re[i], s5_lam_im[i], s5_log_dt[i], s5_b_re[i], s5_b_im[i],
                               s5_c_re[i], s5_c_im[i], s5_d[i], s5_w_glu[i], s5_b_glu[i]))
        ys = [y.reshape(m, -1) for y in (y_rw, y_gla, y_fox, y_s5)]
        x = _merge(x, gain_mix, ys, w_gate[i].astype(BF16), w_branch[i].astype(BF16), w_o[i].astype(BF16))
        x = _ffn(x, _gain_rows(norm_ffb[i]), w_ffb_in[i].astype(BF16), w_ffb_out[i].astype(BF16))
    return x.reshape(b, l, d)
```

```python
import functools
import math

import jax
import jax.numpy as jnp
from jax import lax
from jax.experimental import pallas as pl
from jax.experimental.pallas import tpu as pltpu

F32 = jnp.float32
BF16 = jnp.bfloat16

D_MODEL = 2048
D_FF = 5632
FFN_RES = 0.5
NORM_EPS = 1e-6
HEAD_DIM = 64
CHUNK = 64

RW_HEADS = 8
RW_WIDTH = 512
RW_COLS = 1792
RW_GN_EPS = 64e-5

GLA_HEADS = 4
GLA_DK = 64
GLA_DV = 128
GLA_GATE_RANK = 16
GLA_GATE_NORM = 16.0
GLA_PACKED = 1664

FOX_HEADS = 8
FOX_PACKED = 1664
FOX_AUG = 128

S5_GROUPS = 32
S5_GROUP = 16
S5_STATE = 64
S5_BLOCK = 16
S5_ROW = S5_BLOCK * S5_GROUP

VMEM_LIMIT = 56 * 1024 * 1024


def _cparams(sem):
    return pltpu.CompilerParams(dimension_semantics=sem, vmem_limit_bytes=VMEM_LIMIT)


def _bdot(a, b):
    return jnp.dot(a.astype(BF16), b.astype(BF16), preferred_element_type=F32)


def _bdot_nt(a, b):
    return lax.dot_general(a.astype(BF16), b.astype(BF16), (((1,), (1,)), ((), ())),
                           preferred_element_type=F32)


def _bdot_tn(a, b):
    return lax.dot_general(a.astype(BF16), b.astype(BF16), (((0,), (0,)), ((), ())),
                           preferred_element_type=F32)


def _split3(x):
    h1 = x.astype(BF16)
    r1 = x - h1.astype(F32)
    h2 = r1.astype(BF16)
    h3 = (r1 - h2.astype(F32)).astype(BF16)
    return h1, h2, h3


def _exact_left_dot(m_bf16, x):
    h1, h2, h3 = _split3(x)
    d = lambda h: jnp.dot(m_bf16, h, preferred_element_type=F32)
    return d(h1) + d(h2) + d(h3)


def _exact_right_dot(x, m_bf16):
    h1, h2, h3 = _split3(x)
    d = lambda h: jnp.dot(h, m_bf16, preferred_element_type=F32)
    return d(h1) + d(h2) + d(h3)


def _seg_sums(xs, ones_bd):
    m = xs[0].shape[0]
    parts = []
    for x in xs:
        hi = x.astype(BF16)
        parts += [hi, (x - hi.astype(F32)).astype(BF16)]
    out = jnp.dot(jnp.concatenate(parts, axis=0), ones_bd, preferred_element_type=F32)
    return [out[2 * i * m:(2 * i + 1) * m] + out[(2 * i + 1) * m:(2 * i + 2) * m] for i in range(len(xs))]


def _tri_masks(n):
    row = lax.broadcasted_iota(jnp.int32, (n, n), 0)
    col = lax.broadcasted_iota(jnp.int32, (n, n), 1)
    return row, col


def _sigmoid(x):
    return 1.0 / (1.0 + jnp.exp(-x))


def _softplus(x):
    return jnp.maximum(x, 0.0) + jnp.log(1.0 + jnp.exp(-jnp.abs(x)))


def _log_sigmoid(x):
    return -_softplus(-x)


def _silu(x):
    return x * _sigmoid(x)


def _bmm(a, b):
    return lax.dot_general(a.astype(BF16), b.astype(BF16), (((2,), (1,)), ((0,), (0,))),
                           preferred_element_type=F32)


def _bmm_nt(a, b):
    return lax.dot_general(a.astype(BF16), b.astype(BF16), (((2,), (2,)), ((0,), (0,))),
                           preferred_element_type=F32)


def _bmm_tn(a, b):
    return lax.dot_general(a.astype(BF16), b.astype(BF16), (((1,), (1,)), ((0,), (0,))),
                           preferred_element_type=F32)


def _unit_lower_inverse(a, row, col):
    eye = jnp.where(row == col, 1.0, 0.0).astype(F32)[None]
    ad = jnp.where(((row // 8) == (col // 8))[None], a, 0.0)
    x = eye + ad
    a2 = _bmm(ad, ad)
    x = x + _bmm(a2, x)
    a4 = _bmm(a2, a2)
    x = x + _bmm(a4, x)
    s = 8
    while s < CHUNK:
        off = ((row // (2 * s)) == (col // (2 * s))) & ((row // s) != (col // s))
        e = jnp.where(off[None], a, 0.0)
        x = x + _bmm(x, _bmm(e, x))
        s *= 2
    return x


def _rwkv7_kernel(c_ref, mu_ref, vec_ref, w2_ref, a2_ref, g2_ref, ones_ref, o_ref,
                  prev_sc, state_sc, y_sc, *, tl):
    li = pl.program_id(1)
    nc = tl // CHUNK
    nh = RW_HEADS

    @pl.when(li == 0)
    def _():
        prev_sc[...] = jnp.zeros_like(prev_sc)
        state_sc[...] = jnp.zeros_like(state_sc)

    c = c_ref[0]
    rows = lax.broadcasted_iota(jnp.int32, (tl, 1), 0)
    c_prev = jnp.where(rows == 0, prev_sc[...], pltpu.roll(c, 1, axis=0))
    prev_sc[...] = c[tl - 1:tl, :]
    c = c + (c_prev - c) * mu_ref[...]

    w0, a0, k_k, k_a = vec_ref[0:1, :], vec_ref[1:2, :], vec_ref[2:3, :], vec_ref[3:4, :]
    r_k, ln_g, ln_b = vec_ref[4:5, :], vec_ref[5:6, :], vec_ref[6:7, :]
    ones_bd = ones_ref[...]

    r = c[:, 0:512]
    k = c[:, 512:1024]
    v = c[:, 1024:1536]
    xw = c[:, 1536:1600]
    xa = c[:, 1600:1664]
    xg = c[:, 1664:1792]

    w_log = -_softplus(-(w0 + _bdot(jnp.tanh(xw), w2_ref[...]))) - 0.5
    logw = -jnp.exp(w_log)
    a = _sigmoid(a0 + _bdot(xa, a2_ref[...]))
    g = _bdot(_sigmoid(xg), g2_ref[...])

    kk = k * k_k
    k = k * (1.0 + (a - 1.0) * k_a)
    kk_ss, bonus = _seg_sums([kk * kk, r * k * r_k], ones_bd)
    kk = kk / jnp.maximum(jnp.sqrt(kk_ss), 1e-12)
    av = -kk
    bv = kk * a

    rowt, colt = _tri_masks(tl)
    same_chunk = (rowt // CHUNK) == (colt // CHUNK)
    cum = _exact_left_dot(jnp.where(same_chunk & (rowt >= colt), 1.0, 0.0).astype(BF16), logw)
    tot = jnp.concatenate([jnp.broadcast_to(cum[(j + 1) * CHUNK - 1:(j + 1) * CHUNK], (CHUNK, RW_WIDTH))
                           for j in range(nc)], axis=0)
    e_neg = jnp.exp(-cum)
    e_end = jnp.exp(tot - cum)
    p_c = jnp.exp(tot)

    def heads(x, nrows=CHUNK):
        return jnp.stack([x[j * CHUNK:j * CHUNK + nrows, h * HEAD_DIM:(h + 1) * HEAD_DIM]
                          for j in range(nc) for h in range(nh)])

    a3 = heads(av * jnp.exp(cum - logw))
    r3 = heads(r * jnp.exp(cum))
    v3 = heads(v)
    lhs3 = jnp.concatenate([a3, r3], axis=1)
    rhs3 = jnp.concatenate([heads(bv * e_neg), heads(k * e_neg)], axis=1)
    ber3 = jnp.concatenate([heads(bv * e_end), heads(k * e_end)], axis=1)
    p3 = heads(p_c, 1)

    row, col = _tri_masks(CHUNK)
    strict = (row > col)[None]
    incl = (row >= col)[None]
    aall = _bmm_nt(lhs3, rhs3)
    a_ab = jnp.where(strict, aall[:, :CHUNK, :CHUNK], 0.0)
    a_ak = jnp.where(strict, aall[:, :CHUNK, CHUNK:], 0.0)
    arbk = jnp.concatenate([jnp.where(incl, aall[:, CHUNK:, :CHUNK], 0.0),
                            jnp.where(incl, aall[:, CHUNK:, CHUNK:], 0.0)], axis=2)
    tinv = _unit_lower_inverse(a_ab, row, col)
    wu0 = _bmm(tinv, jnp.concatenate([a3, _bmm(a_ak, v3)], axis=2))
    wr3 = jnp.concatenate([wu0[:, :, :HEAD_DIM], r3], axis=1)
    u03 = wu0[:, :, HEAD_DIM:]

    s = state_sc[...]
    for j in range(nc):
        sel = slice(j * nh, (j + 1) * nh)
        proj = _bmm_nt(wr3[sel], s)
        uv = jnp.concatenate([proj[:, :CHUNK] + u03[sel], v3[sel]], axis=1)
        y3 = proj[:, CHUNK:] + _bmm(arbk[sel], uv)
        s = s * p3[sel] + _bmm_tn(uv, ber3[sel])
        for h in range(nh):
            y_sc[j * CHUNK:(j + 1) * CHUNK, h * HEAD_DIM:(h + 1) * HEAD_DIM] = y3[h]
    state_sc[...] = s

    y = y_sc[...]
    yc = y - _seg_sums([y], ones_bd)[0] * (1.0 / HEAD_DIM)
    var = _seg_sums([yc * yc], ones_bd)[0] * (1.0 / HEAD_DIM)
    y = yc * lax.rsqrt(var + RW_GN_EPS) * ln_g + ln_b
    y = y + bonus * v
    o_ref[0] = (y * g).astype(o_ref.dtype)


def _rwkv7(c_rw, mu, vec, w2, a2, g2, ones64, *, tl=4 * CHUNK):
    b, l, _ = c_rw.shape
    full = lambda shape: pl.BlockSpec(shape, lambda bi, li: (0,) * len(shape))
    return pl.pallas_call(
        functools.partial(_rwkv7_kernel, tl=tl),
        grid=(b, l // tl),
        in_specs=[
            pl.BlockSpec((1, tl, RW_COLS), lambda bi, li: (bi, li, 0)),
            full((1, RW_COLS)), full((8, RW_WIDTH)), full((64, RW_WIDTH)), full((64, RW_WIDTH)),
            full((128, RW_WIDTH)), full((RW_WIDTH, RW_WIDTH)),
        ],
        out_specs=pl.BlockSpec((1, tl, RW_WIDTH), lambda bi, li: (bi, li, 0)),
        out_shape=jax.ShapeDtypeStruct((b, l, RW_WIDTH), BF16),
        scratch_shapes=[
            pltpu.VMEM((1, RW_COLS), F32),
            pltpu.VMEM((RW_HEADS, HEAD_DIM, HEAD_DIM), F32),
            pltpu.VMEM((tl, RW_WIDTH), F32),
        ],
        compiler_params=_cparams(("parallel", "arbitrary")),
        name="rwkv7_mixer",
    )(c_rw, mu, vec, w2, a2, g2, ones64)


def _block_ones(width, block):
    i = jnp.arange(width) // block
    return (i[:, None] == i[None, :]).astype(BF16)


def _rwkv7_params(mu, w0, w2, a0, a2, g2, k_k, k_a, r_k, ln_g, ln_b):
    vec = jnp.stack([w0, a0, k_k, k_a, r_k.reshape(-1), ln_g, ln_b, jnp.zeros_like(w0)]).astype(F32)
    return (mu.reshape(1, -1).astype(F32), vec, w2.astype(BF16), a2.astype(BF16), g2.astype(BF16),
            _block_ones(RW_WIDTH, HEAD_DIM))


def _gla_kernel(c_ref, wg_ref, vec_ref, o_ref, state_sc, o_sc, *, tl):
    li = pl.program_id(1)

    @pl.when(li == 0)
    def _():
        state_sc[...] = jnp.zeros_like(state_sc)

    c = c_ref[0]
    q = c[:, 0:256] * (GLA_DK ** -0.5)
    k = c[:, 256:512]
    v = c[:, 512:1024]
    og = c[:, 1024:1536]
    glr = c[:, 1536:1664]
    b_g2 = vec_ref[0:1, 0:256]
    norm_g = vec_ref[1:2, :]
    log_a = _log_sigmoid(_bdot(glr, wg_ref[...]) + b_g2) * (1.0 / GLA_GATE_NORM)

    nc = tl // CHUNK
    nh = GLA_HEADS
    rowt, colt = _tri_masks(tl)
    same_chunk = (rowt // CHUNK) == (colt // CHUNK)
    cum = _exact_left_dot(jnp.where(same_chunk & (rowt >= colt), 1.0, 0.0).astype(BF16), log_a)
    tot = jnp.concatenate([jnp.broadcast_to(cum[(j + 1) * CHUNK - 1:(j + 1) * CHUNK], (CHUNK, 256))
                           for j in range(nc)], axis=0)

    def heads(x, width, nrows=CHUNK):
        return jnp.stack([x[j * CHUNK:j * CHUNK + nrows, h * width:(h + 1) * width]
                          for j in range(nc) for h in range(nh)])

    qd3 = heads(q * jnp.exp(cum), GLA_DK)
    v3 = heads(v, GLA_DV)
    row, col = _tri_masks(CHUNK)
    att = jnp.where((row >= col)[None], _bmm_nt(qd3, heads(k * jnp.exp(-cum), GLA_DK)), 0.0)
    o3 = _bmm(att, v3)
    kv3 = _bmm_tn(v3, heads(k * jnp.exp(tot - cum), GLA_DK))
    dec3 = heads(jnp.exp(tot), GLA_DK, 1)

    s = state_sc[...]
    for j in range(nc):
        sel = slice(j * nh, (j + 1) * nh)
        oj = o3[sel] + _bmm_nt(qd3[sel], s)
        s = s * dec3[sel] + kv3[sel]
        for h in range(nh):
            o_sc[j * CHUNK:(j + 1) * CHUNK, h * GLA_DV:(h + 1) * GLA_DV] = oj[h]
    state_sc[...] = s

    o = o_sc[...]
    outs = []
    for h in range(GLA_HEADS):
        oh = o[:, h * GLA_DV:(h + 1) * GLA_DV]
        ms = jnp.mean(oh * oh, axis=-1, keepdims=True)
        outs.append(oh * lax.rsqrt(ms + NORM_EPS) * norm_g[:, 0:GLA_DV])
    o = jnp.concatenate(outs, axis=-1)
    o_ref[0] = (o * _silu(og)).astype(o_ref.dtype)


def _gla(c_gla, wg, vec, *, tl=4 * CHUNK):
    b, l, _ = c_gla.shape
    full = lambda shape: pl.BlockSpec(shape, lambda bi, li: (0,) * len(shape))
    return pl.pallas_call(
        functools.partial(_gla_kernel, tl=tl),
        grid=(b, l // tl),
        in_specs=[
            pl.BlockSpec((1, tl, GLA_PACKED), lambda bi, li: (bi, li, 0)),
            full((128, 256)), full((8, 512)),
        ],
        out_specs=pl.BlockSpec((1, tl, 512), lambda bi, li: (bi, li, 0)),
        out_shape=jax.ShapeDtypeStruct((b, l, 512), BF16),
        scratch_shapes=[
            pltpu.VMEM((GLA_HEADS, GLA_DV, GLA_DK), F32),
            pltpu.VMEM((tl, 512), F32),
        ],
        compiler_params=_cparams(("parallel", "arbitrary")),
        name="gla_mixer",
    )(c_gla, wg, vec)


def _gla_params(w_g2, b_g2, norm_g):
    wg = jnp.zeros((128, 256), F32).at[:GLA_GATE_RANK].set(w_g2).astype(BF16)
    vec = jnp.zeros((8, 512), F32).at[0, :256].set(b_g2).at[1, :GLA_DV].set(norm_g)
    return wg, vec


NEG_BIG = -1e30


def _fox_prep_kernel(c_ref, vec_ref, bf_ref, ones_ref, q_ref, k_ref, v_ref, carry_sc, *, tp):
    li = pl.program_id(1)

    @pl.when(li == 0)
    def _():
        carry_sc[...] = jnp.zeros_like(carry_sc)

    c = c_ref[0]
    ones_bd = ones_ref[...]
    q = c[:, 0:512]
    k = c[:, 512:1024]
    v_ref[0] = c[:, 1024:1536].T.astype(v_ref.dtype)
    q_ss, k_ss = _seg_sums([q * q, k * k], ones_bd)
    qn = q * lax.rsqrt(q_ss * (1.0 / HEAD_DIM) + NORM_EPS) * vec_ref[0:1, :] * (HEAD_DIM ** -0.5)
    kn = k * lax.rsqrt(k_ss * (1.0 / HEAD_DIM) + NORM_EPS) * vec_ref[1:2, :]

    log_f = _log_sigmoid(c[:, 1536:1664] + bf_ref[0:1, :])
    row, col = _tri_masks(tp)
    tril_incl = jnp.where(row >= col, 1.0, 0.0).astype(BF16)
    cum = _exact_left_dot(tril_incl, log_f) + carry_sc[...]
    carry_sc[...] = cum[tp - 1:tp, :]

    lane = lax.broadcasted_iota(jnp.int32, (tp, HEAD_DIM), 1)
    for h in range(FOX_HEADS):
        f = jnp.broadcast_to(cum[:, h:h + 1], (tp, HEAD_DIM))
        f1 = f.astype(BF16).astype(F32)
        r1 = f - f1
        f2 = r1.astype(BF16).astype(F32)
        f3 = r1 - f2
        fq = jnp.where(lane == 0, f1, jnp.where(lane == 1, f2, jnp.where(lane == 2, f3, 0.0)))
        aug_q = jnp.where(lane < 3, fq, jnp.where(lane < 6, 1.0, 0.0))
        fk = jnp.where(lane == 3, f1, jnp.where(lane == 4, f2, jnp.where(lane == 5, f3, 0.0)))
        aug_k = jnp.where(lane < 3, 1.0, -fk)
        hs = slice(h * HEAD_DIM, (h + 1) * HEAD_DIM)
        q_ref[0, :, h * FOX_AUG:(h + 1) * FOX_AUG] = jnp.concatenate([qn[:, hs], aug_q], axis=-1).astype(q_ref.dtype)
        k_ref[0, :, h * FOX_AUG:(h + 1) * FOX_AUG] = jnp.concatenate([kn[:, hs], aug_k], axis=-1).astype(k_ref.dtype)


def _fox_attn_kernel(qi_ref, ki_ref, q_ref, k_ref, vt_ref, o_ref, m_sc, l_sc, acc_sc, *, tq, tk, qs_w, ks_w):
    s = pl.program_id(1)
    qi = qi_ref[s]
    ki = ki_ref[s]
    last_k = ((qi + 1) * tq - 1) // tk

    @pl.when(ki == 0)
    def _():
        m_sc[...] = jnp.full_like(m_sc, NEG_BIG)
        l_sc[...] = jnp.zeros_like(l_sc)
        acc_sc[...] = jnp.zeros_like(acc_sc)

    def run(off):
        heads = lambda ref, sl: jnp.stack([ref[0, sl, h * FOX_AUG:(h + 1) * FOX_AUG] for h in range(FOX_HEADS)])
        for qs in range(tq // qs_w):
            qmin, qmax = qs * qs_w, (qs + 1) * qs_w - 1
            units = []
            for ks in range(tk // ks_w):
                kmin = (0 if off is None else off) + ks * ks_w
                if off is None or kmin + ks_w - 1 <= qmin:
                    units.append((ks, kmin, False))
                elif kmin <= qmax:
                    units.append((ks, kmin, True))
            if not units:
                continue
            qsl = slice(qs * qs_w, (qs + 1) * qs_w)
            q3 = heads(q_ref, qsl)
            m, l, acc = m_sc[:, :, qsl], l_sc[:, :, qsl], acc_sc[:, :, qsl]
            for ks, kmin, masked in units:
                ksl = slice(ks * ks_w, (ks + 1) * ks_w)
                sc = lax.dot_general(heads(k_ref, ksl), q3, (((2,), (2,)), ((0,), (0,))),
                                     preferred_element_type=F32)
                if masked:
                    kpos = kmin + lax.broadcasted_iota(jnp.int32, (ks_w, qs_w), 0)
                    qpos = qmin + lax.broadcasted_iota(jnp.int32, (ks_w, qs_w), 1)
                    sc = jnp.where((kpos > qpos)[None], NEG_BIG, sc)
                m_new = jnp.maximum(m, jnp.max(sc, axis=1, keepdims=True))
                alpha = jnp.exp(m - m_new)
                p = jnp.exp(sc - m_new)
                l = alpha * l + jnp.sum(p, axis=1, keepdims=True)
                vt3 = vt_ref[0, :, ksl].reshape(FOX_HEADS, HEAD_DIM, ks_w)
                acc = alpha * acc + lax.dot_general(vt3, p.astype(BF16), (((2,), (1,)), ((0,), (0,))),
                                                    preferred_element_type=F32)
                m = m_new
            m_sc[:, :, qsl], l_sc[:, :, qsl], acc_sc[:, :, qsl] = m, l, acc

    delta = ki * tk - qi * tq

    @pl.when(delta + tk - 1 <= 0)
    def _():
        run(None)

    for off in range(0, tq, tk):
        @pl.when(delta == off)
        def _(off=off):
            run(off)

    @pl.when(ki == last_k)
    def _():
        for hp in range(FOX_HEADS // 2):
            o_t = jnp.concatenate([acc_sc[2 * hp] / l_sc[2 * hp], acc_sc[2 * hp + 1] / l_sc[2 * hp + 1]], axis=0)
            o_ref[0, :, hp * 2 * HEAD_DIM:(hp + 1) * 2 * HEAD_DIM] = o_t.T.astype(o_ref.dtype)


def _fox(c_fox, vec, bf, ones64, *, tp=256, tq=512, tk=256, qs_w=256, ks_w=128):
    b, l, _ = c_fox.shape
    tp, tq, tk = min(tp, l), min(tq, l), min(tk, l)
    full = lambda shape: pl.BlockSpec(shape, lambda bi, li: (0,) * len(shape))
    q_aug, k_aug, v_t = pl.pallas_call(
        functools.partial(_fox_prep_kernel, tp=tp),
        grid=(b, l // tp),
        in_specs=[
            pl.BlockSpec((1, tp, FOX_PACKED), lambda bi, li: (bi, li, 0)),
            full((8, 512)), full((8, 128)), full((512, 512)),
        ],
        out_specs=[
            pl.BlockSpec((1, tp, FOX_HEADS * FOX_AUG), lambda bi, li: (bi, li, 0)),
            pl.BlockSpec((1, tp, FOX_HEADS * FOX_AUG), lambda bi, li: (bi, li, 0)),
            pl.BlockSpec((1, 512, tp), lambda bi, li: (bi, 0, li)),
        ],
        out_shape=[
            jax.ShapeDtypeStruct((b, l, FOX_HEADS * FOX_AUG), BF16),
            jax.ShapeDtypeStruct((b, l, FOX_HEADS * FOX_AUG), BF16),
            jax.ShapeDtypeStruct((b, 512, l), BF16),
        ],
        scratch_shapes=[pltpu.VMEM((1, 128), F32)],
        compiler_params=_cparams(("parallel", "arbitrary")),
        name="fox_prep",
    )(c_fox, vec, bf, ones64)

    pairs = [(qi, ki) for qi in range(l // tq) for ki in range(((qi + 1) * tq - 1) // tk + 1)]
    qi_tab = jnp.asarray([p[0] for p in pairs], jnp.int32)
    ki_tab = jnp.asarray([p[1] for p in pairs], jnp.int32)
    grid_spec = pltpu.PrefetchScalarGridSpec(
        num_scalar_prefetch=2,
        grid=(b, len(pairs)),
        in_specs=[
            pl.BlockSpec((1, tq, FOX_HEADS * FOX_AUG), lambda bi, s, qt, kt: (bi, qt[s], 0)),
            pl.BlockSpec((1, tk, FOX_HEADS * FOX_AUG), lambda bi, s, qt, kt: (bi, kt[s], 0)),
            pl.BlockSpec((1, 512, tk), lambda bi, s, qt, kt: (bi, 0, kt[s])),
        ],
        out_specs=pl.BlockSpec((1, tq, 512), lambda bi, s, qt, kt: (bi, qt[s], 0)),
        scratch_shapes=[
            pltpu.VMEM((FOX_HEADS, 1, tq), F32),
            pltpu.VMEM((FOX_HEADS, 1, tq), F32),
            pltpu.VMEM((FOX_HEADS, HEAD_DIM, tq), F32),
        ],
    )
    return pl.pallas_call(
        functools.partial(_fox_attn_kernel, tq=tq, tk=tk, qs_w=min(qs_w, tq), ks_w=min(ks_w, tk)),
        grid_spec=grid_spec,
        out_shape=jax.ShapeDtypeStruct((b, l, 512), BF16),
        compiler_params=_cparams(("parallel", "arbitrary")),
        name="fox_attention",
    )(qi_tab, ki_tab, q_aug, k_aug, v_t)


def _fox_params(b_f, q_g, k_g):
    vec = jnp.zeros((8, 512), F32).at[0].set(jnp.tile(q_g, FOX_HEADS)).at[1].set(jnp.tile(k_g, FOX_HEADS))
    bf = jnp.zeros((8, 128), F32).at[0, :FOX_HEADS].set(b_f)
    return vec, bf, _block_ones(512, HEAD_DIM)


def _gelu_tanh(x):
    return 0.5 * x * (1.0 + jnp.tanh(math.sqrt(2.0 / math.pi) * (x + 0.044715 * (x * x * x))))


def _s5_core_kernel(u_ref, mt_ref, bend_ref, cpow_ref, apow_ref, d_ref, y_ref, *, nb, levels):
    u = u_ref[0]
    rows = u.shape[0]
    ub = u.astype(BF16)
    y = jnp.dot(ub, mt_ref[0], preferred_element_type=F32)
    x = jnp.dot(ub, bend_ref[0], preferred_element_type=F32)
    blk = lax.broadcasted_iota(jnp.int32, (rows, 1), 0) % nb
    lane = lax.broadcasted_iota(jnp.int32, (1, 2 * S5_STATE), 1)
    for lv in range(levels):
        sh = 1 << lv
        ap = apow_ref[0, lv:lv + 1, :]
        ap_sw = pltpu.roll(ap, S5_STATE, axis=1)
        c1 = jnp.where(lane < S5_STATE, ap, ap_sw)
        c2 = jnp.where(lane < S5_STATE, -ap_sw, ap)
        xs = jnp.where(blk >= sh, pltpu.roll(x, sh, axis=0), 0.0)
        x = x + c1 * xs + c2 * pltpu.roll(xs, S5_STATE, axis=1)
    x_in = jnp.where(blk >= 1, pltpu.roll(x, 1, axis=0), 0.0)
    y = y + jnp.dot(x_in.astype(BF16), cpow_ref[0], preferred_element_type=F32)
    y_ref[0] = y + u * d_ref[0]


def _s5_glu_kernel(y_ref, w_ref, b_ref, o_ref):
    z = _gelu_tanh(y_ref[...])
    o_ref[...] = (z * _sigmoid(_bdot(z, w_ref[...]) + b_ref[0:1, :])).astype(o_ref.dtype)


def _s5(u, mt, bend, cpow, apow, d_row, w_glu, b_glu):
    b, l, _ = u.shape
    nb = l // S5_BLOCK
    rows = b * nb
    levels = max(1, (nb - 1).bit_length())
    ut = u.reshape(b, nb, S5_BLOCK, S5_GROUPS, S5_GROUP).transpose(3, 0, 1, 2, 4).reshape(S5_GROUPS, rows, S5_ROW)
    per_g = lambda shape: pl.BlockSpec((1,) + shape, lambda g: (g, 0, 0))
    yt = pl.pallas_call(
        functools.partial(_s5_core_kernel, nb=nb, levels=levels),
        grid=(S5_GROUPS,),
        in_specs=[per_g((rows, S5_ROW)), per_g((S5_ROW, S5_ROW)), per_g((S5_ROW, 2 * S5_STATE)),
                  per_g((2 * S5_STATE, S5_ROW)), per_g((8, 2 * S5_STATE)), per_g((1, S5_ROW))],
        out_specs=per_g((rows, S5_ROW)),
        out_shape=jax.ShapeDtypeStruct((S5_GROUPS, rows, S5_ROW), F32),
        compiler_params=_cparams(("parallel",)),
        name="s5_core",
    )(ut, mt, bend, cpow, apow, d_row)
    y = yt.reshape(S5_GROUPS, b, nb, S5_BLOCK, S5_GROUP).transpose(1, 2, 3, 0, 4).reshape(b * l, 512)
    tm = min(512, b * l)
    out = pl.pallas_call(
        _s5_glu_kernel,
        grid=(b * l // tm,),
        in_specs=[pl.BlockSpec((tm, 512), lambda i: (i, 0)),
                  pl.BlockSpec((512, 512), lambda i: (0, 0)),
                  pl.BlockSpec((8, 512), lambda i: (0, 0))],
        out_specs=pl.BlockSpec((tm, 512), lambda i: (i, 0)),
        out_shape=jax.ShapeDtypeStruct((b * l, 512), BF16),
        compiler_params=_cparams(("parallel",)),
        name="s5_glu",
    )(y, w_glu, b_glu)
    return out.reshape(b, l, 512)


def _s5_params(lam_re, lam_im, log_dt, b_re, b_im, c_re, c_im, d_skip, w_glu, b_glu):
    t = S5_BLOCK
    lr, li = lam_re.astype(F32), lam_im.astype(F32)
    dt = jnp.exp(log_dt.astype(F32))[:, None]
    mag = jnp.exp(lr * dt)
    ar, ai = mag * jnp.cos(li * dt), mag * jnp.sin(li * dt)
    den = lr * lr + li * li
    zr = ((ar - 1.0) * lr + ai * li) / den
    zi = (ai * lr - (ar - 1.0) * li) / den
    br, bi = b_re.astype(F32), b_im.astype(F32)
    bbr = zr[..., None] * br - zi[..., None] * bi
    bbi = zr[..., None] * bi + zi[..., None] * br
    pr, pi = [jnp.ones_like(ar)], [jnp.zeros_like(ar)]
    for _ in range(t):
        pr, pi = pr + [pr[-1] * ar - pi[-1] * ai], pi + [pr[-1] * ai + pi[-1] * ar]
    pr, pi = jnp.stack(pr), jnp.stack(pi)
    cr, ci = c_re.astype(F32), c_im.astype(F32)
    vr = pr[..., None] * bbr[None] - pi[..., None] * bbi[None]
    vi = pr[..., None] * bbi[None] + pi[..., None] * bbr[None]
    kern = jnp.einsum('ghp,tgpj->tghj', cr, vr[:t]) - jnp.einsum('ghp,tgpj->tghj', ci, vi[:t])
    sidx = jnp.arange(t)
    tau = sidx[None, :] - sidx[:, None]
    blocks = jnp.where((tau >= 0)[..., None, None, None], kern[jnp.clip(tau, 0, t - 1)], 0.0)
    mt = blocks.transpose(2, 0, 4, 1, 3).reshape(S5_GROUPS, S5_ROW, S5_ROW)
    vend_r = vr[t - 1 - sidx]
    vend_i = vi[t - 1 - sidx]
    bend = jnp.concatenate([vend_r.transpose(1, 0, 3, 2).reshape(S5_GROUPS, S5_ROW, S5_STATE),
                            vend_i.transpose(1, 0, 3, 2).reshape(S5_GROUPS, S5_ROW, S5_STATE)], axis=-1)
    p1r, p1i = pr[1:], pi[1:]
    cre = cr[None] * p1r[:, :, None, :] - ci[None] * p1i[:, :, None, :]
    cim = -cr[None] * p1i[:, :, None, :] - ci[None] * p1r[:, :, None, :]
    cpow = jnp.concatenate([cre.transpose(1, 3, 0, 2).reshape(S5_GROUPS, S5_STATE, S5_ROW),
                            cim.transpose(1, 3, 0, 2).reshape(S5_GROUPS, S5_STATE, S5_ROW)], axis=1)
    qr, qi, rows = pr[t], pi[t], []
    for _ in range(8):
        rows.append(jnp.concatenate([qr, qi], axis=-1))
        qr, qi = qr * qr - qi * qi, 2.0 * qr * qi
    apow = jnp.stack(rows, axis=1)
    d_row = jnp.tile(d_skip.astype(F32), (1, t)).reshape(S5_GROUPS, 1, S5_ROW)
    b_pad = jnp.zeros((8, 512), F32).at[0].set(b_glu.astype(F32))
    return (mt.astype(BF16), bend.astype(BF16), cpow.astype(BF16), apow, d_row,
            w_glu.astype(BF16), b_pad)


def _rmsnorm_bf16(x, gain):
    ms = jnp.mean(x * x, axis=-1, keepdims=True)
    return (x * lax.rsqrt(ms + NORM_EPS) * gain).astype(BF16)


def _ffn_kernel(x_ref, g_ref, wg_ref, wu_ref, wo_ref, o_ref, h_sc, acc_sc):
    f = pl.program_id(1)

    @pl.when(f == 0)
    def _():
        h_sc[...] = _rmsnorm_bf16(x_ref[...], g_ref[0:1, :])
        acc_sc[...] = jnp.zeros_like(acc_sc)

    h = h_sc[...]
    gate = jnp.dot(h, wg_ref[...], preferred_element_type=F32)
    up = jnp.dot(h, wu_ref[...], preferred_element_type=F32)
    act = (_silu(gate) * up).astype(BF16)
    acc_sc[...] += jnp.dot(act, wo_ref[...], preferred_element_type=F32)

    @pl.when(f == pl.num_programs(1) - 1)
    def _():
        o_ref[...] = x_ref[...] + FFN_RES * acc_sc[...]


def _ffn(x, gain, w_in, w_out, *, tm=512, tf=512):
    m, d = x.shape
    d_ff = w_out.shape[0]
    tm = min(tm, m)
    nf = d_ff // tf
    return pl.pallas_call(
        _ffn_kernel,
        grid=(m // tm, nf),
        in_specs=[
            pl.BlockSpec((tm, d), lambda i, f: (i, 0)),
            pl.BlockSpec((8, d), lambda i, f: (0, 0)),
            pl.BlockSpec((d, tf), lambda i, f: (0, f)),
            pl.BlockSpec((d, tf), lambda i, f: (0, nf + f)),
            pl.BlockSpec((tf, d), lambda i, f: (f, 0)),
        ],
        out_specs=pl.BlockSpec((tm, d), lambda i, f: (i, 0)),
        out_shape=jax.ShapeDtypeStruct((m, d), F32),
        scratch_shapes=[pltpu.VMEM((tm, d), BF16), pltpu.VMEM((tm, d), F32)],
        compiler_params=_cparams(("parallel", "arbitrary")),
        name="ffn",
    )(x, gain, w_in, w_in, w_out)


def _proj_kernel(x_ref, g_ref, w_rw_ref, w_gla_ref, w_fox_ref, w_s5_ref, rw_ref, gla_ref, fox_ref, s5_ref):
    h = _rmsnorm_bf16(x_ref[...], g_ref[0:1, :])
    rw_ref[...] = jnp.dot(h, w_rw_ref[...], preferred_element_type=F32)
    gla_ref[...] = jnp.dot(h, w_gla_ref[...], preferred_element_type=F32)
    fox_ref[...] = jnp.dot(h, w_fox_ref[...], preferred_element_type=F32)
    s5_ref[...] = jnp.dot(h, w_s5_ref[...], preferred_element_type=F32)


def _proj(x, gain, w_rw, w_gla, w_fox, w_s5, *, tm=256):
    m, d = x.shape
    tm = min(tm, m)
    widths = (RW_COLS, GLA_PACKED, FOX_PACKED, 512)
    resident = lambda n: pl.BlockSpec((d, n), lambda i: (0, 0), pipeline_mode=pl.Buffered(1))
    return pl.pallas_call(
        _proj_kernel,
        grid=(m // tm,),
        in_specs=[pl.BlockSpec((tm, d), lambda i: (i, 0)), pl.BlockSpec((8, d), lambda i: (0, 0))]
                 + [resident(n) for n in widths],
        out_specs=[pl.BlockSpec((tm, n), lambda i: (i, 0)) for n in widths],
        out_shape=[jax.ShapeDtypeStruct((m, n), F32) for n in widths],
        compiler_params=_cparams(("parallel",)),
        name="mixer_in_proj",
    )(x, gain, w_rw, w_gla, w_fox, w_s5)


def _merge_kernel(x_ref, g_ref, y0_ref, y1_ref, y2_ref, y3_ref, wg_ref, wb_ref, wo_ref, o_ref, h_sc, acc_sc):
    n = pl.program_id(1)

    @pl.when(n == 0)
    def _():
        h_sc[...] = _rmsnorm_bf16(x_ref[...], g_ref[0:1, :])
        acc_sc[...] = jnp.zeros_like(acc_sc)

    h = h_sc[...]
    merged = None
    for j, y_ref in enumerate((y0_ref, y1_ref, y2_ref, y3_ref)):
        gate = _sigmoid(jnp.dot(h, wg_ref[j], preferred_element_type=F32))
        term = gate * jnp.dot(y_ref[...], wb_ref[j], preferred_element_type=F32)
        merged = term if merged is None else merged + term
    acc_sc[...] += jnp.dot(merged.astype(BF16), wo_ref[...], preferred_element_type=F32)

    @pl.when(n == pl.num_programs(1) - 1)
    def _():
        o_ref[...] = x_ref[...] + acc_sc[...]


def _merge(x, gain, ys, w_gate, w_branch, w_o, *, tm=512, tn=256):
    m, d = x.shape
    tm = min(tm, m)
    bw = ys[0].shape[1]
    return pl.pallas_call(
        _merge_kernel,
        grid=(m // tm, d // tn),
        in_specs=[pl.BlockSpec((tm, d), lambda i, n: (i, 0)), pl.BlockSpec((8, d), lambda i, n: (0, 0))]
                 + [pl.BlockSpec((tm, bw), lambda i, n: (i, 0)) for _ in ys]
                 + [pl.BlockSpec((4, d, tn), lambda i, n: (0, 0, n)),
                    pl.BlockSpec((4, bw, tn), lambda i, n: (0, 0, n)),
                    pl.BlockSpec((tn, d), lambda i, n: (n, 0))],
        out_specs=pl.BlockSpec((tm, d), lambda i, n: (i, 0)),
        out_shape=jax.ShapeDtypeStruct((m, d), F32),
        scratch_shapes=[pltpu.VMEM((tm, d), BF16), pltpu.VMEM((tm, d), F32)],
        compiler_params=_cparams(("parallel", "arbitrary")),
        name="merge",
    )(x, gain, *ys, w_gate, w_branch, w_o)


def _gain_rows(g):
    return jnp.broadcast_to(g.astype(F32)[None, :], (8, g.shape[0]))


def _pack_w_in(w):
    d = w.shape[0]
    o = 0
    w_rw = w[:, o:o + RW_COLS]
    o += RW_COLS
    q, k, v, glr, og = (w[:, o:o + 256], w[:, o + 256:o + 512], w[:, o + 512:o + 1024],
                        w[:, o + 1024:o + 1040], w[:, o + 1040:o + 1552])
    w_gla = jnp.concatenate([q, k, v, og, glr, jnp.zeros((d, 128 - GLA_GATE_RANK), w.dtype)], axis=1)
    o += 1552
    w_fox = jnp.concatenate([w[:, o:o + 1544], jnp.zeros((d, 128 - FOX_HEADS), w.dtype)], axis=1)
    o += 1544
    w_s5 = w[:, o:o + 512]
    return [t.astype(BF16) for t in (w_rw, w_gla, w_fox, w_s5)]


def kernel(x, norm_ffa, w_ffa_in, w_ffa_out, norm_mix, w_in, rw_mu, rw_w0, rw_w2, rw_a0, rw_a2, rw_g2, rw_kk, rw_ka, rw_rk, rw_ln_g, rw_ln_b, gla_wg2, gla_bg2, gla_norm_g, fox_bf, fox_qg, fox_kg, s5_lam_re, s5_lam_im, s5_log_dt, s5_b_re, s5_b_im, s5_c_re, s5_c_im, s5_d, s5_w_glu, s5_b_glu, w_gate, w_branch, w_o, norm_ffb, w_ffb_in, w_ffb_out):
    b, l, d = x.shape
    m = b * l
    x = x.reshape(m, d).astype(F32)
    for i in range(norm_ffa.shape[0]):
        x = _ffn(x, _gain_rows(norm_ffa[i]), w_ffa_in[i].astype(BF16), w_ffa_out[i].astype(BF16))
        gain_mix = _gain_rows(norm_mix[i])
        c_rw, c_gla, c_fox, c_s5 = _proj(x, gain_mix, *_pack_w_in(w_in[i]))
        y_rw = _rwkv7(c_rw.reshape(b, l, -1),
                      *_rwkv7_params(rw_mu[i], rw_w0[i], rw_w2[i], rw_a0[i], rw_a2[i], rw_g2[i],
                                     rw_kk[i], rw_ka[i], rw_rk[i], rw_ln_g[i], rw_ln_b[i]))
        y_gla = _gla(c_gla.reshape(b, l, -1), *_gla_params(gla_wg2[i], gla_bg2[i], gla_norm_g[i]))
        y_fox = _fox(c_fox.reshape(b, l, -1), *_fox_params(fox_bf[i], fox_qg[i], fox_kg[i]))
        y_s5 = _s5(c_s5.reshape(b, l, -1),
                   *_s5_params(s5_lam_re[i], s5_lam_im[i], s5_log_dt[i], s5_b_re[i], s5_b_im[i],
                               s5_c_re[i], s5_c_im[i], s5_d[i], s5_w_glu[i], s5_b_glu[i]))
        ys = [y.reshape(m, -1) for y in (y_rw, y_gla, y_fox, y_s5)]
        x = _merge(x, gain_mix, ys, w_gate[i].astype(BF16), w_branch[i].astype(BF16), w_o[i].astype(BF16))
        x = _ffn(x, _gain_rows(norm_ffb[i]), w_ffb_in[i].astype(BF16), w_ffb_out[i].astype(BF16))
    return x.reshape(b, l, d)
```

```python
import functools
import math

import jax
import jax.numpy as jnp
from jax import lax
from jax.experimental import pallas as pl
from jax.experimental.pallas import tpu as pltpu

F32 = jnp.float32
BF16 = jnp.bfloat16

D_MODEL = 2048
D_FF = 5632
FFN_RES = 0.5
NORM_EPS = 1e-6
HEAD_DIM = 64
CHUNK = 64

RW_HEADS = 8
RW_WIDTH = 512
RW_COLS = 1792
RW_GN_EPS = 64e-5

GLA_HEADS = 4
GLA_DK = 64
GLA_DV = 128
GLA_GATE_RANK = 16
GLA_GATE_NORM = 16.0
GLA_PACKED = 1664

FOX_HEADS = 8
FOX_PACKED = 1664
FOX_AUG = 128

S5_GROUPS = 32
S5_GROUP = 16
S5_STATE = 64
S5_BLOCK = 16
S5_ROW = S5_BLOCK * S5_GROUP

VMEM_LIMIT = 58 * 1024 * 1024


def _cparams(sem):
    return pltpu.CompilerParams(dimension_semantics=sem, vmem_limit_bytes=VMEM_LIMIT)


def _bdot(a, b):
    return jnp.dot(a.astype(BF16), b.astype(BF16), preferred_element_type=F32)


def _bdot_nt(a, b):
    return lax.dot_general(a.astype(BF16), b.astype(BF16), (((1,), (1,)), ((), ())),
                           preferred_element_type=F32)


def _bdot_tn(a, b):
    return lax.dot_general(a.astype(BF16), b.astype(BF16), (((0,), (0,)), ((), ())),
                           preferred_element_type=F32)


def _split3(x):
    h1 = x.astype(BF16)
    r1 = x - h1.astype(F32)
    h2 = r1.astype(BF16)
    h3 = (r1 - h2.astype(F32)).astype(BF16)
    return h1, h2, h3


def _exact_left_dot(m_bf16, x):
    h1, h2, h3 = _split3(x)
    d = lambda h: jnp.dot(m_bf16, h, preferred_element_type=F32)
    return d(h1) + d(h2) + d(h3)


def _exact_right_dot(x, m_bf16):
    h1, h2, h3 = _split3(x)
    d = lambda h: jnp.dot(h, m_bf16, preferred_element_type=F32)
    return d(h1) + d(h2) + d(h3)


def _seg_sums(xs, ones_bd):
    m = xs[0].shape[0]
    parts = []
    for x in xs:
        hi = x.astype(BF16)
        parts += [hi, (x - hi.astype(F32)).astype(BF16)]
    out = jnp.dot(jnp.concatenate(parts, axis=0), ones_bd, preferred_element_type=F32)
    return [out[2 * i * m:(2 * i + 1) * m] + out[(2 * i + 1) * m:(2 * i + 2) * m] for i in range(len(xs))]


def _tri_masks(n):
    row = lax.broadcasted_iota(jnp.int32, (n, n), 0)
    col = lax.broadcasted_iota(jnp.int32, (n, n), 1)
    return row, col


def _sigmoid(x):
    return 1.0 / (1.0 + jnp.exp(-x))


def _softplus(x):
    return jnp.maximum(x, 0.0) + jnp.log(1.0 + jnp.exp(-jnp.abs(x)))


def _log_sigmoid(x):
    return -_softplus(-x)


def _silu(x):
    return x * _sigmoid(x)


def _bmm(a, b):
    return lax.dot_general(a.astype(BF16), b.astype(BF16), (((2,), (1,)), ((0,), (0,))),
                           preferred_element_type=F32)


def _bmm_nt(a, b):
    return lax.dot_general(a.astype(BF16), b.astype(BF16), (((2,), (2,)), ((0,), (0,))),
                           preferred_element_type=F32)


def _bmm_tn(a, b):
    return lax.dot_general(a.astype(BF16), b.astype(BF16), (((1,), (1,)), ((0,), (0,))),
                           preferred_element_type=F32)


def _unit_lower_inverse(a, row, col):
    eye = jnp.where(row == col, 1.0, 0.0).astype(F32)[None]
    ad = jnp.where(((row // 8) == (col // 8))[None], a, 0.0)
    x = eye + ad
    a2 = _bmm(ad, ad)
    x = x + _bmm(a2, x)
    a4 = _bmm(a2, a2)
    x = x + _bmm(a4, x)
    s = 8
    while s < CHUNK:
        off = ((row // (2 * s)) == (col // (2 * s))) & ((row // s) != (col // s))
        e = jnp.where(off[None], a, 0.0)
        x = x + _bmm(x, _bmm(e, x))
        s *= 2
    return x


def _rwkv7_kernel(c_ref, mu_ref, vec_ref, w2_ref, a2_ref, g2_ref, ones_ref, o_ref,
                  prev_sc, state_sc, y_sc, *, tl):
    li = pl.program_id(1)
    nc = tl // CHUNK
    nh = RW_HEADS

    @pl.when(li == 0)
    def _():
        prev_sc[...] = jnp.zeros_like(prev_sc)
        state_sc[...] = jnp.zeros_like(state_sc)

    c = c_ref[0]
    rows = lax.broadcasted_iota(jnp.int32, (tl, 1), 0)
    c_prev = jnp.where(rows == 0, prev_sc[...], pltpu.roll(c, 1, axis=0))
    prev_sc[...] = c[tl - 1:tl, :]
    c = c + (c_prev - c) * mu_ref[...]

    w0, a0, k_k, k_a = vec_ref[0:1, :], vec_ref[1:2, :], vec_ref[2:3, :], vec_ref[3:4, :]
    r_k, ln_g, ln_b = vec_ref[4:5, :], vec_ref[5:6, :], vec_ref[6:7, :]
    ones_bd = ones_ref[...]

    r = c[:, 0:512]
    k = c[:, 512:1024]
    v = c[:, 1024:1536]
    xw = c[:, 1536:1600]
    xa = c[:, 1600:1664]
    xg = c[:, 1664:1792]

    w_log = -_softplus(-(w0 + _bdot(jnp.tanh(xw), w2_ref[...]))) - 0.5
    logw = -jnp.exp(w_log)
    a = _sigmoid(a0 + _bdot(xa, a2_ref[...]))
    g = _bdot(_sigmoid(xg), g2_ref[...])

    kk = k * k_k
    k = k * (1.0 + (a - 1.0) * k_a)
    kk_ss, bonus = _seg_sums([kk * kk, r * k * r_k], ones_bd)
    kk = kk / jnp.maximum(jnp.sqrt(kk_ss), 1e-12)
    av = -kk
    bv = kk * a

    rowt, colt = _tri_masks(tl)
    same_chunk = (rowt // CHUNK) == (colt // CHUNK)
    cum = _exact_left_dot(jnp.where(same_chunk & (rowt >= colt), 1.0, 0.0).astype(BF16), logw)
    tot = jnp.concatenate([jnp.broadcast_to(cum[(j + 1) * CHUNK - 1:(j + 1) * CHUNK], (CHUNK, RW_WIDTH))
                           for j in range(nc)], axis=0)
    e_neg = jnp.exp(-cum)
    e_end = jnp.exp(tot - cum)
    p_c = jnp.exp(tot)

    def heads(x, nrows=CHUNK):
        return jnp.stack([x[j * CHUNK:j * CHUNK + nrows, h * HEAD_DIM:(h + 1) * HEAD_DIM]
                          for j in range(nc) for h in range(nh)])

    a3 = heads(av * jnp.exp(cum - logw))
    r3 = heads(r * jnp.exp(cum))
    v3 = heads(v)
    lhs3 = jnp.concatenate([a3, r3], axis=1)
    rhs3 = jnp.concatenate([heads(bv * e_neg), heads(k * e_neg)], axis=1)
    ber3 = jnp.concatenate([heads(bv * e_end), heads(k * e_end)], axis=1)
    p3 = heads(p_c, 1)

    row, col = _tri_masks(CHUNK)
    strict = (row > col)[None]
    incl = (row >= col)[None]
    aall = _bmm_nt(lhs3, rhs3)
    a_ab = jnp.where(strict, aall[:, :CHUNK, :CHUNK], 0.0)
    a_ak = jnp.where(strict, aall[:, :CHUNK, CHUNK:], 0.0)
    arbk = jnp.concatenate([jnp.where(incl, aall[:, CHUNK:, :CHUNK], 0.0),
                            jnp.where(incl, aall[:, CHUNK:, CHUNK:], 0.0)], axis=2)
    tinv = _unit_lower_inverse(a_ab, row, col)
    wu0 = _bmm(tinv, jnp.concatenate([a3, _bmm(a_ak, v3)], axis=2))
    wr3 = jnp.concatenate([wu0[:, :, :HEAD_DIM], r3], axis=1)
    u03 = wu0[:, :, HEAD_DIM:]

    s = state_sc[...]
    for j in range(nc):
        sel = slice(j * nh, (j + 1) * nh)
        proj = _bmm_nt(wr3[sel], s)
        uv = jnp.concatenate([proj[:, :CHUNK] + u03[sel], v3[sel]], axis=1)
        y3 = proj[:, CHUNK:] + _bmm(arbk[sel], uv)
        s = s * p3[sel] + _bmm_tn(uv, ber3[sel])
        for h in range(nh):
            y_sc[j * CHUNK:(j + 1) * CHUNK, h * HEAD_DIM:(h + 1) * HEAD_DIM] = y3[h]
    state_sc[...] = s

    y = y_sc[...]
    yc = y - _seg_sums([y], ones_bd)[0] * (1.0 / HEAD_DIM)
    var = _seg_sums([yc * yc], ones_bd)[0] * (1.0 / HEAD_DIM)
    y = yc * lax.rsqrt(var + RW_GN_EPS) * ln_g + ln_b
    y = y + bonus * v
    o_ref[0] = (y * g).astype(o_ref.dtype)


def _rwkv7(c_rw, mu, vec, w2, a2, g2, ones64, *, tl=4 * CHUNK):
    b, l, _ = c_rw.shape
    full = lambda shape: pl.BlockSpec(shape, lambda bi, li: (0,) * len(shape))
    return pl.pallas_call(
        functools.partial(_rwkv7_kernel, tl=tl),
        grid=(b, l // tl),
        in_specs=[
            pl.BlockSpec((1, tl, RW_COLS), lambda bi, li: (bi, li, 0)),
            full((1, RW_COLS)), full((8, RW_WIDTH)), full((64, RW_WIDTH)), full((64, RW_WIDTH)),
            full((128, RW_WIDTH)), full((RW_WIDTH, RW_WIDTH)),
        ],
        out_specs=pl.BlockSpec((1, tl, RW_WIDTH), lambda bi, li: (bi, li, 0)),
        out_shape=jax.ShapeDtypeStruct((b, l, RW_WIDTH), BF16),
        scratch_shapes=[
            pltpu.VMEM((1, RW_COLS), F32),
            pltpu.VMEM((RW_HEADS, HEAD_DIM, HEAD_DIM), F32),
            pltpu.VMEM((tl, RW_WIDTH), F32),
        ],
        compiler_params=_cparams(("parallel", "arbitrary")),
        name="rwkv7_mixer",
    )(c_rw, mu, vec, w2, a2, g2, ones64)


def _block_ones(width, block):
    i = jnp.arange(width) // block
    return (i[:, None] == i[None, :]).astype(BF16)


def _rwkv7_params(mu, w0, w2, a0, a2, g2, k_k, k_a, r_k, ln_g, ln_b):
    vec = jnp.stack([w0, a0, k_k, k_a, r_k.reshape(-1), ln_g, ln_b, jnp.zeros_like(w0)]).astype(F32)
    return (mu.reshape(1, -1).astype(F32), vec, w2.astype(BF16), a2.astype(BF16), g2.astype(BF16),
            _block_ones(RW_WIDTH, HEAD_DIM))


def _gla_kernel(c_ref, wg_ref, vec_ref, o_ref, state_sc, o_sc, *, tl):
    li = pl.program_id(1)

    @pl.when(li == 0)
    def _():
        state_sc[...] = jnp.zeros_like(state_sc)

    c = c_ref[0]
    q = c[:, 0:256] * (GLA_DK ** -0.5)
    k = c[:, 256:512]
    v = c[:, 512:1024]
    og = c[:, 1024:1536]
    glr = c[:, 1536:1664]
    b_g2 = vec_ref[0:1, 0:256]
    norm_g = vec_ref[1:2, :]
    log_a = _log_sigmoid(_bdot(glr, wg_ref[...]) + b_g2) * (1.0 / GLA_GATE_NORM)

    nc = tl // CHUNK
    nh = GLA_HEADS
    rowt, colt = _tri_masks(tl)
    same_chunk = (rowt // CHUNK) == (colt // CHUNK)
    cum = _exact_left_dot(jnp.where(same_chunk & (rowt >= colt), 1.0, 0.0).astype(BF16), log_a)
    tot = jnp.concatenate([jnp.broadcast_to(cum[(j + 1) * CHUNK - 1:(j + 1) * CHUNK], (CHUNK, 256))
                           for j in range(nc)], axis=0)

    def heads(x, width, nrows=CHUNK):
        return jnp.stack([x[j * CHUNK:j * CHUNK + nrows, h * width:(h + 1) * width]
                          for j in range(nc) for h in range(nh)])

    qd3 = heads(q * jnp.exp(cum), GLA_DK)
    v3 = heads(v, GLA_DV)
    row, col = _tri_masks(CHUNK)
    att = jnp.where((row >= col)[None], _bmm_nt(qd3, heads(k * jnp.exp(-cum), GLA_DK)), 0.0)
    o3 = _bmm(att, v3)
    kv3 = _bmm_tn(v3, heads(k * jnp.exp(tot - cum), GLA_DK))
    dec3 = heads(jnp.exp(tot), GLA_DK, 1)

    s = state_sc[...]
    for j in range(nc):
        sel = slice(j * nh, (j + 1) * nh)
        oj = o3[sel] + _bmm_nt(qd3[sel], s)
        s = s * dec3[sel] + kv3[sel]
        for h in range(nh):
            o_sc[j * CHUNK:(j + 1) * CHUNK, h * GLA_DV:(h + 1) * GLA_DV] = oj[h]
    state_sc[...] = s

    o = o_sc[...]
    outs = []
    for h in range(GLA_HEADS):
        oh = o[:, h * GLA_DV:(h + 1) * GLA_DV]
        ms = jnp.mean(oh * oh, axis=-1, keepdims=True)
        outs.append(oh * lax.rsqrt(ms + NORM_EPS) * norm_g[:, 0:GLA_DV])
    o = jnp.concatenate(outs, axis=-1)
    o_ref[0] = (o * _silu(og)).astype(o_ref.dtype)


def _gla(c_gla, wg, vec, *, tl=4 * CHUNK):
    b, l, _ = c_gla.shape
    full = lambda shape: pl.BlockSpec(shape, lambda bi, li: (0,) * len(shape))
    return pl.pallas_call(
        functools.partial(_gla_kernel, tl=tl),
        grid=(b, l // tl),
        in_specs=[
            pl.BlockSpec((1, tl, GLA_PACKED), lambda bi, li: (bi, li, 0)),
            full((128, 256)), full((8, 512)),
        ],
        out_specs=pl.BlockSpec((1, tl, 512), lambda bi, li: (bi, li, 0)),
        out_shape=jax.ShapeDtypeStruct((b, l, 512), BF16),
        scratch_shapes=[
            pltpu.VMEM((GLA_HEADS, GLA_DV, GLA_DK), F32),
            pltpu.VMEM((tl, 512), F32),
        ],
        compiler_params=_cparams(("parallel", "arbitrary")),
        name="gla_mixer",
    )(c_gla, wg, vec)


def _gla_params(w_g2, b_g2, norm_g):
    wg = jnp.zeros((128, 256), F32).at[:GLA_GATE_RANK].set(w_g2).astype(BF16)
    vec = jnp.zeros((8, 512), F32).at[0, :256].set(b_g2).at[1, :GLA_DV].set(norm_g)
    return wg, vec


NEG_BIG = -1e30


def _fox_prep_kernel(c_ref, vec_ref, bf_ref, ones_ref, q_ref, k_ref, v_ref, carry_sc, *, tp):
    li = pl.program_id(1)

    @pl.when(li == 0)
    def _():
        carry_sc[...] = jnp.zeros_like(carry_sc)

    c = c_ref[0]
    ones_bd = ones_ref[...]
    q = c[:, 0:512]
    k = c[:, 512:1024]
    v_ref[0] = c[:, 1024:1536].T.astype(v_ref.dtype)
    q_ss, k_ss = _seg_sums([q * q, k * k], ones_bd)
    qn = q * lax.rsqrt(q_ss * (1.0 / HEAD_DIM) + NORM_EPS) * vec_ref[0:1, :] * (HEAD_DIM ** -0.5)
    kn = k * lax.rsqrt(k_ss * (1.0 / HEAD_DIM) + NORM_EPS) * vec_ref[1:2, :]

    log_f = _log_sigmoid(c[:, 1536:1664] + bf_ref[0:1, :])
    row, col = _tri_masks(tp)
    tril_incl = jnp.where(row >= col, 1.0, 0.0).astype(BF16)
    cum = _exact_left_dot(tril_incl, log_f) + carry_sc[...]
    carry_sc[...] = cum[tp - 1:tp, :]

    lane = lax.broadcasted_iota(jnp.int32, (tp, HEAD_DIM), 1)
    for h in range(FOX_HEADS):
        f = jnp.broadcast_to(cum[:, h:h + 1], (tp, HEAD_DIM))
        f1 = f.astype(BF16).astype(F32)
        r1 = f - f1
        f2 = r1.astype(BF16).astype(F32)
        f3 = r1 - f2
        fq = jnp.where(lane == 0, f1, jnp.where(lane == 1, f2, jnp.where(lane == 2, f3, 0.0)))
        aug_q = jnp.where(lane < 3, fq, jnp.where(lane < 6, 1.0, 0.0))
        fk = jnp.where(lane == 3, f1, jnp.where(lane == 4, f2, jnp.where(lane == 5, f3, 0.0)))
        aug_k = jnp.where(lane < 3, 1.0, -fk)
        hs = slice(h * HEAD_DIM, (h + 1) * HEAD_DIM)
        q_ref[0, :, h * FOX_AUG:(h + 1) * FOX_AUG] = jnp.concatenate([qn[:, hs], aug_q], axis=-1).astype(q_ref.dtype)
        k_ref[0, :, h * FOX_AUG:(h + 1) * FOX_AUG] = jnp.concatenate([kn[:, hs], aug_k], axis=-1).astype(k_ref.dtype)


def _fox_attn_kernel(qi_ref, ki_ref, q_ref, k_ref, vt_ref, o_ref, m_sc, l_sc, acc_sc, *, tq, tk, qs_w, ks_w):
    s = pl.program_id(1)
    qi = qi_ref[s]
    ki = ki_ref[s]
    last_k = ((qi + 1) * tq - 1) // tk

    @pl.when(ki == 0)
    def _():
        m_sc[...] = jnp.full_like(m_sc, NEG_BIG)
        l_sc[...] = jnp.zeros_like(l_sc)
        acc_sc[...] = jnp.zeros_like(acc_sc)

    def run(off):
        heads = lambda ref, sl: jnp.stack([ref[0, sl, h * FOX_AUG:(h + 1) * FOX_AUG] for h in range(FOX_HEADS)])
        for qs in range(tq // qs_w):
            qmin, qmax = qs * qs_w, (qs + 1) * qs_w - 1
            units = []
            for ks in range(tk // ks_w):
                kmin = (0 if off is None else off) + ks * ks_w
                if off is None or kmin + ks_w - 1 <= qmin:
                    units.append((ks, kmin, False))
                elif kmin <= qmax:
                    units.append((ks, kmin, True))
            if not units:
                continue
            qsl = slice(qs * qs_w, (qs + 1) * qs_w)
            q3 = heads(q_ref, qsl)
            m, l, acc = m_sc[:, :, qsl], l_sc[:, :, qsl], acc_sc[:, :, qsl]
            for ks, kmin, masked in units:
                ksl = slice(ks * ks_w, (ks + 1) * ks_w)
                sc = lax.dot_general(heads(k_ref, ksl), q3, (((2,), (2,)), ((0,), (0,))),
                                     preferred_element_type=F32)
                if masked:
                    kpos = kmin + lax.broadcasted_iota(jnp.int32, (ks_w, qs_w), 0)
                    qpos = qmin + lax.broadcasted_iota(jnp.int32, (ks_w, qs_w), 1)
                    sc = jnp.where((kpos > qpos)[None], NEG_BIG, sc)
                m_new = jnp.maximum(m, jnp.max(sc, axis=1, keepdims=True))
                alpha = jnp.exp(m - m_new)
                p = jnp.exp(sc - m_new)
                l = alpha * l + jnp.sum(p, axis=1, keepdims=True)
                vt3 = vt_ref[0, :, ksl].reshape(FOX_HEADS, HEAD_DIM, ks_w)
                acc = alpha * acc + lax.dot_general(vt3, p.astype(BF16), (((2,), (1,)), ((0,), (0,))),
                                                    preferred_element_type=F32)
                m = m_new
            m_sc[:, :, qsl], l_sc[:, :, qsl], acc_sc[:, :, qsl] = m, l, acc

    delta = ki * tk - qi * tq

    @pl.when(delta + tk - 1 <= 0)
    def _():
        run(None)

    for off in range(0, tq, tk):
        @pl.when(delta == off)
        def _(off=off):
            run(off)

    @pl.when(ki == last_k)
    def _():
        for hp in range(FOX_HEADS // 2):
            o_t = jnp.concatenate([acc_sc[2 * hp] / l_sc[2 * hp], acc_sc[2 * hp + 1] / l_sc[2 * hp + 1]], axis=0)
            o_ref[0, :, hp * 2 * HEAD_DIM:(hp + 1) * 2 * HEAD_DIM] = o_t.T.astype(o_ref.dtype)


def _fox(c_fox, vec, bf, ones64, *, tp=256, tq=512, tk=256, qs_w=256, ks_w=128):
    b, l, _ = c_fox.shape
    tp, tq, tk = min(tp, l), min(tq, l), min(tk, l)
    full = lambda shape: pl.BlockSpec(shape, lambda bi, li: (0,) * len(shape))
    q_aug, k_aug, v_t = pl.pallas_call(
        functools.partial(_fox_prep_kernel, tp=tp),
        grid=(b, l // tp),
        in_specs=[
            pl.BlockSpec((1, tp, FOX_PACKED), lambda bi, li: (bi, li, 0)),
            full((8, 512)), full((8, 128)), full((512, 512)),
        ],
        out_specs=[
            pl.BlockSpec((1, tp, FOX_HEADS * FOX_AUG), lambda bi, li: (bi, li, 0)),
            pl.BlockSpec((1, tp, FOX_HEADS * FOX_AUG), lambda bi, li: (bi, li, 0)),
            pl.BlockSpec((1, 512, tp), lambda bi, li: (bi, 0, li)),
        ],
        out_shape=[
            jax.ShapeDtypeStruct((b, l, FOX_HEADS * FOX_AUG), BF16),
            jax.ShapeDtypeStruct((b, l, FOX_HEADS * FOX_AUG), BF16),
            jax.ShapeDtypeStruct((b, 512, l), BF16),
        ],
        scratch_shapes=[pltpu.VMEM((1, 128), F32)],
        compiler_params=_cparams(("parallel", "arbitrary")),
        name="fox_prep",
    )(c_fox, vec, bf, ones64)

    pairs = [(qi, ki) for qi in range(l // tq) for ki in range(((qi + 1) * tq - 1) // tk + 1)]
    qi_tab = jnp.asarray([p[0] for p in pairs], jnp.int32)
    ki_tab = jnp.asarray([p[1] for p in pairs], jnp.int32)
    grid_spec = pltpu.PrefetchScalarGridSpec(
        num_scalar_prefetch=2,
        grid=(b, len(pairs)),
        in_specs=[
            pl.BlockSpec((1, tq, FOX_HEADS * FOX_AUG), lambda bi, s, qt, kt: (bi, qt[s], 0)),
            pl.BlockSpec((1, tk, FOX_HEADS * FOX_AUG), lambda bi, s, qt, kt: (bi, kt[s], 0)),
            pl.BlockSpec((1, 512, tk), lambda bi, s, qt, kt: (bi, 0, kt[s])),
        ],
        out_specs=pl.BlockSpec((1, tq, 512), lambda bi, s, qt, kt: (bi, qt[s], 0)),
        scratch_shapes=[
            pltpu.VMEM((FOX_HEADS, 1, tq), F32),
            pltpu.VMEM((FOX_HEADS, 1, tq), F32),
            pltpu.VMEM((FOX_HEADS, HEAD_DIM, tq), F32),
        ],
    )
    return pl.pallas_call(
        functools.partial(_fox_attn_kernel, tq=tq, tk=tk, qs_w=min(qs_w, tq), ks_w=min(ks_w, tk)),
        grid_spec=grid_spec,
        out_shape=jax.ShapeDtypeStruct((b, l, 512), BF16),
        compiler_params=_cparams(("parallel", "arbitrary")),
        name="fox_attention",
    )(qi_tab, ki_tab, q_aug, k_aug, v_t)


def _fox_params(b_f, q_g, k_g):
    vec = jnp.zeros((8, 512), F32).at[0].set(jnp.tile(q_g, FOX_HEADS)).at[1].set(jnp.tile(k_g, FOX_HEADS))
    bf = jnp.zeros((8, 128), F32).at[0, :FOX_HEADS].set(b_f)
    return vec, bf, _block_ones(512, HEAD_DIM)


def _gelu_tanh(x):
    return 0.5 * x * (1.0 + jnp.tanh(math.sqrt(2.0 / math.pi) * (x + 0.044715 * (x * x * x))))


def _s5_core_kernel(u_ref, mt_ref, bend_ref, cpow_ref, apow_ref, d_ref, y_ref, *, nb, levels):
    u = u_ref[0]
    rows = u.shape[0]
    ub = u.astype(BF16)
    y = jnp.dot(ub, mt_ref[0], preferred_element_type=F32)
    x = jnp.dot(ub, bend_ref[0], preferred_element_type=F32)
    blk = lax.broadcasted_iota(jnp.int32, (rows, 1), 0) % nb
    lane = lax.broadcasted_iota(jnp.int32, (1, 2 * S5_STATE), 1)
    for lv in range(levels):
        sh = 1 << lv
        ap = apow_ref[0, lv:lv + 1, :]
        ap_sw = pltpu.roll(ap, S5_STATE, axis=1)
        c1 = jnp.where(lane < S5_STATE, ap, ap_sw)
        c2 = jnp.where(lane < S5_STATE, -ap_sw, ap)
        xs = jnp.where(blk >= sh, pltpu.roll(x, sh, axis=0), 0.0)
        x = x + c1 * xs + c2 * pltpu.roll(xs, S5_STATE, axis=1)
    x_in = jnp.where(blk >= 1, pltpu.roll(x, 1, axis=0), 0.0)
    y = y + jnp.dot(x_in.astype(BF16), cpow_ref[0], preferred_element_type=F32)
    y_ref[0] = (y + u.astype(F32) * d_ref[0]).astype(y_ref.dtype)


def _s5_glu_kernel(y_ref, w_ref, b_ref, o_ref):
    z = _gelu_tanh(y_ref[...].astype(F32))
    o_ref[...] = (z * _sigmoid(_bdot(z, w_ref[...]) + b_ref[0:1, :])).astype(o_ref.dtype)


def _s5(u, mt, bend, cpow, apow, d_row, w_glu, b_glu):
    b, l, _ = u.shape
    nb = l // S5_BLOCK
    rows = b * nb
    levels = max(1, (nb - 1).bit_length())
    ut = u.astype(BF16).reshape(b, nb, S5_BLOCK, S5_GROUPS, S5_GROUP).transpose(3, 0, 1, 2, 4)
    ut = ut.reshape(S5_GROUPS, rows, S5_ROW)
    per_g = lambda shape: pl.BlockSpec((1,) + shape, lambda g: (g, 0, 0))
    yt = pl.pallas_call(
        functools.partial(_s5_core_kernel, nb=nb, levels=levels),
        grid=(S5_GROUPS,),
        in_specs=[per_g((rows, S5_ROW)), per_g((S5_ROW, S5_ROW)), per_g((S5_ROW, 2 * S5_STATE)),
                  per_g((2 * S5_STATE, S5_ROW)), per_g((8, 2 * S5_STATE)), per_g((1, S5_ROW))],
        out_specs=per_g((rows, S5_ROW)),
        out_shape=jax.ShapeDtypeStruct((S5_GROUPS, rows, S5_ROW), BF16),
        compiler_params=_cparams(("parallel",)),
        name="s5_core",
    )(ut, mt, bend, cpow, apow, d_row)
    y = yt.reshape(S5_GROUPS, b, nb, S5_BLOCK, S5_GROUP).transpose(1, 2, 3, 0, 4).reshape(b * l, 512)
    tm = min(512, b * l)
    out = pl.pallas_call(
        _s5_glu_kernel,
        grid=(b * l // tm,),
        in_specs=[pl.BlockSpec((tm, 512), lambda i: (i, 0)),
                  pl.BlockSpec((512, 512), lambda i: (0, 0)),
                  pl.BlockSpec((8, 512), lambda i: (0, 0))],
        out_specs=pl.BlockSpec((tm, 512), lambda i: (i, 0)),
        out_shape=jax.ShapeDtypeStruct((b * l, 512), BF16),
        compiler_params=_cparams(("parallel",)),
        name="s5_glu",
    )(y, w_glu, b_glu)
    return out.reshape(b, l, 512)


def _s5_params(lam_re, lam_im, log_dt, b_re, b_im, c_re, c_im, d_skip, w_glu, b_glu):
    t = S5_BLOCK
    lr, li = lam_re.astype(F32), lam_im.astype(F32)
    dt = jnp.exp(log_dt.astype(F32))[:, None]
    mag = jnp.exp(lr * dt)
    ar, ai = mag * jnp.cos(li * dt), mag * jnp.sin(li * dt)
    den = lr * lr + li * li
    zr = ((ar - 1.0) * lr + ai * li) / den
    zi = (ai * lr - (ar - 1.0) * li) / den
    br, bi = b_re.astype(F32), b_im.astype(F32)
    bbr = zr[..., None] * br - zi[..., None] * bi
    bbi = zr[..., None] * bi + zi[..., None] * br
    pr, pi = [jnp.ones_like(ar)], [jnp.zeros_like(ar)]
    for _ in range(t):
        pr, pi = pr + [pr[-1] * ar - pi[-1] * ai], pi + [pr[-1] * ai + pi[-1] * ar]
    pr, pi = jnp.stack(pr), jnp.stack(pi)
    cr, ci = c_re.astype(F32), c_im.astype(F32)
    vr = pr[..., None] * bbr[None] - pi[..., None] * bbi[None]
    vi = pr[..., None] * bbi[None] + pi[..., None] * bbr[None]
    kern = jnp.einsum('ghp,tgpj->tghj', cr, vr[:t]) - jnp.einsum('ghp,tgpj->tghj', ci, vi[:t])
    sidx = jnp.arange(t)
    tau = sidx[None, :] - sidx[:, None]
    blocks = jnp.where((tau >= 0)[..., None, None, None], kern[jnp.clip(tau, 0, t - 1)], 0.0)
    mt = blocks.transpose(2, 0, 4, 1, 3).reshape(S5_GROUPS, S5_ROW, S5_ROW)
    vend_r = vr[t - 1 - sidx]
    vend_i = vi[t - 1 - sidx]
    bend = jnp.concatenate([vend_r.transpose(1, 0, 3, 2).reshape(S5_GROUPS, S5_ROW, S5_STATE),
                            vend_i.transpose(1, 0, 3, 2).reshape(S5_GROUPS, S5_ROW, S5_STATE)], axis=-1)
    p1r, p1i = pr[1:], pi[1:]
    cre = cr[None] * p1r[:, :, None, :] - ci[None] * p1i[:, :, None, :]
    cim = -cr[None] * p1i[:, :, None, :] - ci[None] * p1r[:, :, None, :]
    cpow = jnp.concatenate([cre.transpose(1, 3, 0, 2).reshape(S5_GROUPS, S5_STATE, S5_ROW),
                            cim.transpose(1, 3, 0, 2).reshape(S5_GROUPS, S5_STATE, S5_ROW)], axis=1)
    qr, qi, rows = pr[t], pi[t], []
    for _ in range(8):
        rows.append(jnp.concatenate([qr, qi], axis=-1))
        qr, qi = qr * qr - qi * qi, 2.0 * qr * qi
    apow = jnp.stack(rows, axis=1)
    d_row = jnp.tile(d_skip.astype(F32), (1, t)).reshape(S5_GROUPS, 1, S5_ROW)
    b_pad = jnp.zeros((8, 512), F32).at[0].set(b_glu.astype(F32))
    return (mt.astype(BF16), bend.astype(BF16), cpow.astype(BF16), apow, d_row,
            w_glu.astype(BF16), b_pad)


def _rmsnorm_bf16(x, gain):
    ms = jnp.mean(x * x, axis=-1, keepdims=True)
    return (x * lax.rsqrt(ms + NORM_EPS) * gain).astype(BF16)


def _ffn_kernel(x_ref, g_ref, wg_ref, wu_ref, wo_ref, o_ref, h_sc):
    f = pl.program_id(1)

    @pl.when(f == 0)
    def _():
        h_sc[...] = _rmsnorm_bf16(x_ref[...], g_ref[0:1, :])

    h = h_sc[...]
    gate = jnp.dot(h, wg_ref[...], preferred_element_type=F32)
    up = jnp.dot(h, wu_ref[...], preferred_element_type=F32)
    act = (_silu(gate) * up).astype(BF16)
    part = jnp.dot(act, wo_ref[...], preferred_element_type=F32)

    @pl.when(f == 0)
    def _():
        o_ref[...] = part

    @pl.when(f > 0)
    def _():
        o_ref[...] += part

    @pl.when(f == pl.num_programs(1) - 1)
    def _():
        o_ref[...] = x_ref[...] + FFN_RES * o_ref[...]


def _ffn(x, gain, w_in, w_out, layer, *, tm=1024, tf=512):
    m, d = x.shape
    d_ff = w_out.shape[1]
    tm = min(tm, m)
    nf = d_ff // tf
    return pl.pallas_call(
        _ffn_kernel,
        grid=(m // tm, nf),
        in_specs=[
            pl.BlockSpec((tm, d), lambda i, f: (i, 0)),
            pl.BlockSpec((8, d), lambda i, f: (0, 0)),
            pl.BlockSpec((None, d, tf), lambda i, f: (layer, 0, f)),
            pl.BlockSpec((None, d, tf), lambda i, f: (layer, 0, nf + f)),
            pl.BlockSpec((None, tf, d), lambda i, f: (layer, f, 0)),
        ],
        out_specs=pl.BlockSpec((tm, d), lambda i, f: (i, 0)),
        out_shape=jax.ShapeDtypeStruct((m, d), F32),
        scratch_shapes=[pltpu.VMEM((tm, d), BF16)],
        compiler_params=_cparams(("parallel", "arbitrary")),
        name="ffn",
    )(x, gain, w_in, w_in, w_out)


def _proj_kernel(x_ref, g_ref, w_rw_ref, w_gla_ref, w_fox_ref, w_s5_ref, rw_ref, gla_ref, fox_ref, s5_ref):
    h = _rmsnorm_bf16(x_ref[...], g_ref[0:1, :])
    rw_ref[...] = jnp.dot(h, w_rw_ref[...], preferred_element_type=F32)
    gla_ref[...] = jnp.dot(h, w_gla_ref[...], preferred_element_type=F32)
    fox_ref[...] = jnp.dot(h, w_fox_ref[...], preferred_element_type=F32)
    s5_ref[...] = jnp.dot(h, w_s5_ref[...], preferred_element_type=F32).astype(s5_ref.dtype)


def _proj(x, gain, w_rw, w_gla, w_fox, w_s5, *, tm=256):
    m, d = x.shape
    tm = min(tm, m)
    widths = (RW_COLS, GLA_PACKED, FOX_PACKED, 512)
    resident = lambda n: pl.BlockSpec((d, n), lambda i: (0, 0), pipeline_mode=pl.Buffered(1))
    return pl.pallas_call(
        _proj_kernel,
        grid=(m // tm,),
        in_specs=[pl.BlockSpec((tm, d), lambda i: (i, 0)), pl.BlockSpec((8, d), lambda i: (0, 0))]
                 + [resident(n) for n in widths],
        out_specs=[pl.BlockSpec((tm, n), lambda i: (i, 0)) for n in widths],
        out_shape=[jax.ShapeDtypeStruct((m, n), F32) for n in widths[:3]]
                  + [jax.ShapeDtypeStruct((m, widths[3]), BF16)],
        compiler_params=_cparams(("parallel",)),
        name="mixer_in_proj",
    )(x, gain, w_rw, w_gla, w_fox, w_s5)


def _merge_kernel(x_ref, g_ref, y0_ref, y1_ref, y2_ref, y3_ref, wg_ref, wb_ref, wo_ref, o_ref, h_sc, acc_sc):
    n = pl.program_id(1)

    @pl.when(n == 0)
    def _():
        h_sc[...] = _rmsnorm_bf16(x_ref[...], g_ref[0:1, :])
        acc_sc[...] = jnp.zeros_like(acc_sc)

    h = h_sc[...]
    merged = None
    for j, y_ref in enumerate((y0_ref, y1_ref, y2_ref, y3_ref)):
        gate = _sigmoid(jnp.dot(h, wg_ref[j], preferred_element_type=F32))
        term = gate * jnp.dot(y_ref[...], wb_ref[j], preferred_element_type=F32)
        merged = term if merged is None else merged + term
    acc_sc[...] += jnp.dot(merged.astype(BF16), wo_ref[...], preferred_element_type=F32)

    @pl.when(n == pl.num_programs(1) - 1)
    def _():
        o_ref[...] = x_ref[...] + acc_sc[...]


def _merge(x, gain, ys, w_gate, w_branch, w_o, layer, *, tm=512, tn=256):
    m, d = x.shape
    tm = min(tm, m)
    bw = ys[0].shape[1]
    return pl.pallas_call(
        _merge_kernel,
        grid=(m // tm, d // tn),
        in_specs=[pl.BlockSpec((tm, d), lambda i, n: (i, 0)), pl.BlockSpec((8, d), lambda i, n: (0, 0))]
                 + [pl.BlockSpec((tm, bw), lambda i, n: (i, 0)) for _ in ys]
                 + [pl.BlockSpec((None, 4, d, tn), lambda i, n: (layer, 0, 0, n)),
                    pl.BlockSpec((None, 4, bw, tn), lambda i, n: (layer, 0, 0, n)),
                    pl.BlockSpec((None, tn, d), lambda i, n: (layer, n, 0))],
        out_specs=pl.BlockSpec((tm, d), lambda i, n: (i, 0)),
        out_shape=jax.ShapeDtypeStruct((m, d), F32),
        scratch_shapes=[pltpu.VMEM((tm, d), BF16), pltpu.VMEM((tm, d), F32)],
        compiler_params=_cparams(("parallel", "arbitrary")),
        name="merge",
    )(x, gain, *ys, w_gate, w_branch, w_o)


def _gain_rows(g):
    return jnp.broadcast_to(g.astype(F32)[None, :], (8, g.shape[0]))


def _pack_w_in(w):
    d = w.shape[0]
    o = 0
    w_rw = w[:, o:o + RW_COLS]
    o += RW_COLS
    q, k, v, glr, og = (w[:, o:o + 256], w[:, o + 256:o + 512], w[:, o + 512:o + 1024],
                        w[:, o + 1024:o + 1040], w[:, o + 1040:o + 1552])
    w_gla = jnp.concatenate([q, k, v, og, glr, jnp.zeros((d, 128 - GLA_GATE_RANK), w.dtype)], axis=1)
    o += 1552
    w_fox = jnp.concatenate([w[:, o:o + 1544], jnp.zeros((d, 128 - FOX_HEADS), w.dtype)], axis=1)
    o += 1544
    w_s5 = w[:, o:o + 512]
    return [t.astype(BF16) for t in (w_rw, w_gla, w_fox, w_s5)]


def kernel(x, norm_ffa, w_ffa_in, w_ffa_out, norm_mix, w_in, rw_mu, rw_w0, rw_w2, rw_a0, rw_a2, rw_g2, rw_kk, rw_ka, rw_rk, rw_ln_g, rw_ln_b, gla_wg2, gla_bg2, gla_norm_g, fox_bf, fox_qg, fox_kg, s5_lam_re, s5_lam_im, s5_log_dt, s5_b_re, s5_b_im, s5_c_re, s5_c_im, s5_d, s5_w_glu, s5_b_glu, w_gate, w_branch, w_o, norm_ffb, w_ffb_in, w_ffb_out):
    b, l, d = x.shape
    m = b * l
    x = x.reshape(m, d).astype(F32)
    w_ffa_in, w_ffa_out, w_ffb_in, w_ffb_out, w_gate, w_branch, w_o = (
        w.astype(BF16) for w in (w_ffa_in, w_ffa_out, w_ffb_in, w_ffb_out, w_gate, w_branch, w_o))
    for i in range(norm_ffa.shape[0]):
        x = _ffn(x, _gain_rows(norm_ffa[i]), w_ffa_in, w_ffa_out, i)
        gain_mix = _gain_rows(norm_mix[i])
        c_rw, c_gla, c_fox, c_s5 = _proj(x, gain_mix, *_pack_w_in(w_in[i]))
        y_rw = _rwkv7(c_rw.reshape(b, l, -1),
                      *_rwkv7_params(rw_mu[i], rw_w0[i], rw_w2[i], rw_a0[i], rw_a2[i], rw_g2[i],
                                     rw_kk[i], rw_ka[i], rw_rk[i], rw_ln_g[i], rw_ln_b[i]))
        y_gla = _gla(c_gla.reshape(b, l, -1), *_gla_params(gla_wg2[i], gla_bg2[i], gla_norm_g[i]))
        y_fox = _fox(c_fox.reshape(b, l, -1), *_fox_params(fox_bf[i], fox_qg[i], fox_kg[i]))
        y_s5 = _s5(c_s5.reshape(b, l, -1),
                   *_s5_params(s5_lam_re[i], s5_lam_im[i], s5_log_dt[i], s5_b_re[i], s5_b_im[i],
                               s5_c_re[i], s5_c_im[i], s5_d[i], s5_w_glu[i], s5_b_glu[i]))
        ys = [y.reshape(m, -1) for y in (y_rw, y_gla, y_fox, y_s5)]
        x = _merge(x, gain_mix, ys, w_gate, w_branch, w_o, i)
        x = _ffn(x, _gain_rows(norm_ffb[i]), w_ffb_in, w_ffb_out, i)
    return x.reshape(b, l, d)
```

```python
import functools
import math

import jax
import jax.numpy as jnp
from jax import lax
from jax.experimental import pallas as pl
from jax.experimental.pallas import tpu as pltpu

F32 = jnp.float32
BF16 = jnp.bfloat16

D_MODEL = 2048
D_FF = 5632
FFN_RES = 0.5
NORM_EPS = 1e-6
HEAD_DIM = 64
CHUNK = 64

RW_HEADS = 8
RW_WIDTH = 512
RW_COLS = 1792
RW_GN_EPS = 64e-5

GLA_HEADS = 4
GLA_DK = 64
GLA_DV = 128
GLA_GATE_RANK = 16
GLA_GATE_NORM = 16.0
GLA_PACKED = 1664

FOX_HEADS = 8
FOX_PACKED = 1664
FOX_AUG = 128

S5_GROUPS = 32
S5_GROUP = 16
S5_STATE = 64
S5_BLOCK = 16
S5_ROW = S5_BLOCK * S5_GROUP

VMEM_LIMIT = 58 * 1024 * 1024


def _cparams(sem):
    return pltpu.CompilerParams(dimension_semantics=sem, vmem_limit_bytes=VMEM_LIMIT)


def _bdot(a, b):
    return jnp.dot(a.astype(BF16), b.astype(BF16), preferred_element_type=F32)


def _bdot_nt(a, b):
    return lax.dot_general(a.astype(BF16), b.astype(BF16), (((1,), (1,)), ((), ())),
                           preferred_element_type=F32)


def _bdot_tn(a, b):
    return lax.dot_general(a.astype(BF16), b.astype(BF16), (((0,), (0,)), ((), ())),
                           preferred_element_type=F32)


def _split3(x):
    h1 = x.astype(BF16)
    r1 = x - h1.astype(F32)
    h2 = r1.astype(BF16)
    h3 = (r1 - h2.astype(F32)).astype(BF16)
    return h1, h2, h3


def _exact_left_dot(m_bf16, x):
    h1, h2, h3 = _split3(x)
    d = lambda h: jnp.dot(m_bf16, h, preferred_element_type=F32)
    return d(h1) + d(h2) + d(h3)


def _exact_right_dot(x, m_bf16):
    h1, h2, h3 = _split3(x)
    d = lambda h: jnp.dot(h, m_bf16, preferred_element_type=F32)
    return d(h1) + d(h2) + d(h3)


def _seg_sums(xs, ones_bd):
    m = xs[0].shape[0]
    parts = []
    for x in xs:
        hi = x.astype(BF16)
        parts += [hi, (x - hi.astype(F32)).astype(BF16)]
    out = jnp.dot(jnp.concatenate(parts, axis=0), ones_bd, preferred_element_type=F32)
    return [out[2 * i * m:(2 * i + 1) * m] + out[(2 * i + 1) * m:(2 * i + 2) * m] for i in range(len(xs))]


def _tri_masks(n):
    row = lax.broadcasted_iota(jnp.int32, (n, n), 0)
    col = lax.broadcasted_iota(jnp.int32, (n, n), 1)
    return row, col


def _sigmoid(x):
    return 1.0 / (1.0 + jnp.exp(-x))


def _softplus(x):
    return jnp.maximum(x, 0.0) + jnp.log(1.0 + jnp.exp(-jnp.abs(x)))


def _log_sigmoid(x):
    return -_softplus(-x)


def _silu(x):
    return x * _sigmoid(x)


def _bmm(a, b):
    return lax.dot_general(a.astype(BF16), b.astype(BF16), (((2,), (1,)), ((0,), (0,))),
                           preferred_element_type=F32)


def _bmm_nt(a, b):
    return lax.dot_general(a.astype(BF16), b.astype(BF16), (((2,), (2,)), ((0,), (0,))),
                           preferred_element_type=F32)


def _bmm_tn(a, b):
    return lax.dot_general(a.astype(BF16), b.astype(BF16), (((1,), (1,)), ((0,), (0,))),
                           preferred_element_type=F32)


def _unit_lower_inverse(a, row, col):
    eye = jnp.where(row == col, 1.0, 0.0).astype(F32)[None]
    ad = jnp.where(((row // 8) == (col // 8))[None], a, 0.0)
    x = eye + ad
    a2 = _bmm(ad, ad)
    x = x + _bmm(a2, x)
    a4 = _bmm(a2, a2)
    x = x + _bmm(a4, x)
    s = 8
    while s < CHUNK:
        off = ((row // (2 * s)) == (col // (2 * s))) & ((row // s) != (col // s))
        e = jnp.where(off[None], a, 0.0)
        x = x + _bmm(x, _bmm(e, x))
        s *= 2
    return x


def _rwkv7_kernel(c_ref, mu_ref, vec_ref, w2_ref, a2_ref, g2_ref, ones_ref, o_ref,
                  prev_sc, state_sc, y_sc, *, tl):
    li = pl.program_id(1)
    nc = tl // CHUNK
    nh = RW_HEADS

    @pl.when(li == 0)
    def _():
        prev_sc[...] = jnp.zeros_like(prev_sc)
        state_sc[...] = jnp.zeros_like(state_sc)

    c = c_ref[0]
    rows = lax.broadcasted_iota(jnp.int32, (tl, 1), 0)
    c_prev = jnp.where(rows == 0, prev_sc[...], pltpu.roll(c, 1, axis=0))
    prev_sc[...] = c[tl - 1:tl, :]
    c = c + (c_prev - c) * mu_ref[...]

    w0, a0, k_k, k_a = vec_ref[0:1, :], vec_ref[1:2, :], vec_ref[2:3, :], vec_ref[3:4, :]
    r_k, ln_g, ln_b = vec_ref[4:5, :], vec_ref[5:6, :], vec_ref[6:7, :]
    ones_bd = ones_ref[...]

    r = c[:, 0:512]
    k = c[:, 512:1024]
    v = c[:, 1024:1536]
    xw = c[:, 1536:1600]
    xa = c[:, 1600:1664]
    xg = c[:, 1664:1792]

    w_log = -_softplus(-(w0 + _bdot(jnp.tanh(xw), w2_ref[...]))) - 0.5
    logw = -jnp.exp(w_log)
    a = _sigmoid(a0 + _bdot(xa, a2_ref[...]))
    g = _bdot(_sigmoid(xg), g2_ref[...])

    kk = k * k_k
    k = k * (1.0 + (a - 1.0) * k_a)
    kk_ss, bonus = _seg_sums([kk * kk, r * k * r_k], ones_bd)
    kk = kk / jnp.maximum(jnp.sqrt(kk_ss), 1e-12)
    av = -kk
    bv = kk * a

    rowt, colt = _tri_masks(tl)
    same_chunk = (rowt // CHUNK) == (colt // CHUNK)
    cum = _exact_left_dot(jnp.where(same_chunk & (rowt >= colt), 1.0, 0.0).astype(BF16), logw)
    tot = jnp.concatenate([jnp.broadcast_to(cum[(j + 1) * CHUNK - 1:(j + 1) * CHUNK], (CHUNK, RW_WIDTH))
                           for j in range(nc)], axis=0)
    e_neg = jnp.exp(-cum)
    e_end = jnp.exp(tot - cum)
    p_c = jnp.exp(tot)

    def heads(x, nrows=CHUNK):
        return jnp.stack([x[j * CHUNK:j * CHUNK + nrows, h * HEAD_DIM:(h + 1) * HEAD_DIM]
                          for j in range(nc) for h in range(nh)])

    a3 = heads(av * jnp.exp(cum - logw))
    r3 = heads(r * jnp.exp(cum))
    v3 = heads(v)
    lhs3 = jnp.concatenate([a3, r3], axis=1)
    rhs3 = jnp.concatenate([heads(bv * e_neg), heads(k * e_neg)], axis=1)
    ber3 = jnp.concatenate([heads(bv * e_end), heads(k * e_end)], axis=1)
    p3 = heads(p_c, 1)

    row, col = _tri_masks(CHUNK)
    strict = (row > col)[None]
    incl = (row >= col)[None]
    aall = _bmm_nt(lhs3, rhs3)
    a_ab = jnp.where(strict, aall[:, :CHUNK, :CHUNK], 0.0)
    a_ak = jnp.where(strict, aall[:, :CHUNK, CHUNK:], 0.0)
    arbk = jnp.concatenate([jnp.where(incl, aall[:, CHUNK:, :CHUNK], 0.0),
                            jnp.where(incl, aall[:, CHUNK:, CHUNK:], 0.0)], axis=2)
    tinv = _unit_lower_inverse(a_ab, row, col)
    wu0 = _bmm(tinv, jnp.concatenate([a3, _bmm(a_ak, v3)], axis=2))
    wr3 = jnp.concatenate([wu0[:, :, :HEAD_DIM], r3], axis=1)
    u03 = wu0[:, :, HEAD_DIM:]

    s = state_sc[...]
    for j in range(nc):
        sel = slice(j * nh, (j + 1) * nh)
        proj = _bmm_nt(wr3[sel], s)
        uv = jnp.concatenate([proj[:, :CHUNK] + u03[sel], v3[sel]], axis=1)
        y3 = proj[:, CHUNK:] + _bmm(arbk[sel], uv)
        s = s * p3[sel] + _bmm_tn(uv, ber3[sel])
        for h in range(nh):
            y_sc[j * CHUNK:(j + 1) * CHUNK, h * HEAD_DIM:(h + 1) * HEAD_DIM] = y3[h]
    state_sc[...] = s

    y = y_sc[...]
    yc = y - _seg_sums([y], ones_bd)[0] * (1.0 / HEAD_DIM)
    var = _seg_sums([yc * yc], ones_bd)[0] * (1.0 / HEAD_DIM)
    y = yc * lax.rsqrt(var + RW_GN_EPS) * ln_g + ln_b
    y = y + bonus * v
    o_ref[0] = (y * g).astype(o_ref.dtype)


def _rwkv7(c_rw, mu, vec, w2, a2, g2, ones64, *, tl=4 * CHUNK):
    b, l, _ = c_rw.shape
    full = lambda shape: pl.BlockSpec(shape, lambda bi, li: (0,) * len(shape))
    return pl.pallas_call(
        functools.partial(_rwkv7_kernel, tl=tl),
        grid=(b, l // tl),
        in_specs=[
            pl.BlockSpec((1, tl, RW_COLS), lambda bi, li: (bi, li, 0)),
            full((1, RW_COLS)), full((8, RW_WIDTH)), full((64, RW_WIDTH)), full((64, RW_WIDTH)),
            full((128, RW_WIDTH)), full((RW_WIDTH, RW_WIDTH)),
        ],
        out_specs=pl.BlockSpec((1, tl, RW_WIDTH), lambda bi, li: (bi, li, 0)),
        out_shape=jax.ShapeDtypeStruct((b, l, RW_WIDTH), BF16),
        scratch_shapes=[
            pltpu.VMEM((1, RW_COLS), F32),
            pltpu.VMEM((RW_HEADS, HEAD_DIM, HEAD_DIM), F32),
            pltpu.VMEM((tl, RW_WIDTH), F32),
        ],
        compiler_params=_cparams(("parallel", "arbitrary")),
        name="rwkv7_mixer",
    )(c_rw, mu, vec, w2, a2, g2, ones64)


def _block_ones(width, block):
    i = jnp.arange(width) // block
    return (i[:, None] == i[None, :]).astype(BF16)


def _rwkv7_params(mu, w0, w2, a0, a2, g2, k_k, k_a, r_k, ln_g, ln_b):
    vec = jnp.stack([w0, a0, k_k, k_a, r_k.reshape(-1), ln_g, ln_b, jnp.zeros_like(w0)]).astype(F32)
    return (mu.reshape(1, -1).astype(F32), vec, w2.astype(BF16), a2.astype(BF16), g2.astype(BF16),
            _block_ones(RW_WIDTH, HEAD_DIM))


def _gla_kernel(c_ref, wg_ref, vec_ref, o_ref, state_sc, o_sc, *, tl):
    li = pl.program_id(1)

    @pl.when(li == 0)
    def _():
        state_sc[...] = jnp.zeros_like(state_sc)

    c = c_ref[0]
    q = c[:, 0:256] * (GLA_DK ** -0.5)
    k = c[:, 256:512]
    v = c[:, 512:1024]
    og = c[:, 1024:1536]
    glr = c[:, 1536:1664]
    b_g2 = vec_ref[0:1, 0:256]
    norm_g = vec_ref[1:2, :]
    log_a = _log_sigmoid(_bdot(glr, wg_ref[...]) + b_g2) * (1.0 / GLA_GATE_NORM)

    nc = tl // CHUNK
    nh = GLA_HEADS
    rowt, colt = _tri_masks(tl)
    same_chunk = (rowt // CHUNK) == (colt // CHUNK)
    cum = _exact_left_dot(jnp.where(same_chunk & (rowt >= colt), 1.0, 0.0).astype(BF16), log_a)
    tot = jnp.concatenate([jnp.broadcast_to(cum[(j + 1) * CHUNK - 1:(j + 1) * CHUNK], (CHUNK, 256))
                           for j in range(nc)], axis=0)

    def heads(x, width, nrows=CHUNK):
        return jnp.stack([x[j * CHUNK:j * CHUNK + nrows, h * width:(h + 1) * width]
                          for j in range(nc) for h in range(nh)])

    qd3 = heads(q * jnp.exp(cum), GLA_DK)
    v3 = heads(v, GLA_DV)
    row, col = _tri_masks(CHUNK)
    att = jnp.where((row >= col)[None], _bmm_nt(qd3, heads(k * jnp.exp(-cum), GLA_DK)), 0.0)
    o3 = _bmm(att, v3)
    kv3 = _bmm_tn(v3, heads(k * jnp.exp(tot - cum), GLA_DK))
    dec3 = heads(jnp.exp(tot), GLA_DK, 1)

    s = state_sc[...]
    for j in range(nc):
        sel = slice(j * nh, (j + 1) * nh)
        oj = o3[sel] + _bmm_nt(qd3[sel], s)
        s = s * dec3[sel] + kv3[sel]
        for h in range(nh):
            o_sc[j * CHUNK:(j + 1) * CHUNK, h * GLA_DV:(h + 1) * GLA_DV] = oj[h]
    state_sc[...] = s

    o = o_sc[...]
    outs = []
    for h in range(GLA_HEADS):
        oh = o[:, h * GLA_DV:(h + 1) * GLA_DV]
        ms = jnp.mean(oh * oh, axis=-1, keepdims=True)
        outs.append(oh * lax.rsqrt(ms + NORM_EPS) * norm_g[:, 0:GLA_DV])
    o = jnp.concatenate(outs, axis=-1)
    o_ref[0] = (o * _silu(og)).astype(o_ref.dtype)


def _gla(c_gla, wg, vec, *, tl=4 * CHUNK):
    b, l, _ = c_gla.shape
    full = lambda shape: pl.BlockSpec(shape, lambda bi, li: (0,) * len(shape))
    return pl.pallas_call(
        functools.partial(_gla_kernel, tl=tl),
        grid=(b, l // tl),
        in_specs=[
            pl.BlockSpec((1, tl, GLA_PACKED), lambda bi, li: (bi, li, 0)),
            full((128, 256)), full((8, 512)),
        ],
        out_specs=pl.BlockSpec((1, tl, 512), lambda bi, li: (bi, li, 0)),
        out_shape=jax.ShapeDtypeStruct((b, l, 512), BF16),
        scratch_shapes=[
            pltpu.VMEM((GLA_HEADS, GLA_DV, GLA_DK), F32),
            pltpu.VMEM((tl, 512), F32),
        ],
        compiler_params=_cparams(("parallel", "arbitrary")),
        name="gla_mixer",
    )(c_gla, wg, vec)


def _gla_params(w_g2, b_g2, norm_g):
    wg = jnp.zeros((128, 256), F32).at[:GLA_GATE_RANK].set(w_g2).astype(BF16)
    vec = jnp.zeros((8, 512), F32).at[0, :256].set(b_g2).at[1, :GLA_DV].set(norm_g)
    return wg, vec


NEG_BIG = -1e30


def _fox_prep_kernel(c_ref, vec_ref, bf_ref, ones_ref, q_ref, k_ref, v_ref, carry_sc, *, tp):
    li = pl.program_id(1)

    @pl.when(li == 0)
    def _():
        carry_sc[...] = jnp.zeros_like(carry_sc)

    c = c_ref[0]
    ones_bd = ones_ref[...]
    q = c[:, 0:512]
    k = c[:, 512:1024]
    v_ref[0] = c[:, 1024:1536].T.astype(v_ref.dtype)
    q_ss, k_ss = _seg_sums([q * q, k * k], ones_bd)
    qn = q * lax.rsqrt(q_ss * (1.0 / HEAD_DIM) + NORM_EPS) * vec_ref[0:1, :] * (HEAD_DIM ** -0.5)
    kn = k * lax.rsqrt(k_ss * (1.0 / HEAD_DIM) + NORM_EPS) * vec_ref[1:2, :]

    log_f = _log_sigmoid(c[:, 1536:1664] + bf_ref[0:1, :])
    row, col = _tri_masks(tp)
    tril_incl = jnp.where(row >= col, 1.0, 0.0).astype(BF16)
    cum = _exact_left_dot(tril_incl, log_f) + carry_sc[...]
    carry_sc[...] = cum[tp - 1:tp, :]

    lane = lax.broadcasted_iota(jnp.int32, (tp, HEAD_DIM), 1)
    for h in range(FOX_HEADS):
        f = jnp.broadcast_to(cum[:, h:h + 1], (tp, HEAD_DIM))
        f1 = f.astype(BF16).astype(F32)
        r1 = f - f1
        f2 = r1.astype(BF16).astype(F32)
        f3 = r1 - f2
        fq = jnp.where(lane == 0, f1, jnp.where(lane == 1, f2, jnp.where(lane == 2, f3, 0.0)))
        aug_q = jnp.where(lane < 3, fq, jnp.where(lane < 6, 1.0, 0.0))
        fk = jnp.where(lane == 3, f1, jnp.where(lane == 4, f2, jnp.where(lane == 5, f3, 0.0)))
        aug_k = jnp.where(lane < 3, 1.0, -fk)
        hs = slice(h * HEAD_DIM, (h + 1) * HEAD_DIM)
        q_ref[0, :, h * FOX_AUG:(h + 1) * FOX_AUG] = jnp.concatenate([qn[:, hs], aug_q], axis=-1).astype(q_ref.dtype)
        k_ref[0, :, h * FOX_AUG:(h + 1) * FOX_AUG] = jnp.concatenate([kn[:, hs], aug_k], axis=-1).astype(k_ref.dtype)


def _fox_attn_kernel(qi_ref, ki_ref, q_ref, k_ref, vt_ref, o_ref, m_sc, l_sc, acc_sc, *, tq, tk, qs_w, ks_w):
    s = pl.program_id(1)
    qi = qi_ref[s]
    ki = ki_ref[s]
    last_k = ((qi + 1) * tq - 1) // tk

    @pl.when(ki == 0)
    def _():
        m_sc[...] = jnp.full_like(m_sc, NEG_BIG)
        l_sc[...] = jnp.zeros_like(l_sc)
        acc_sc[...] = jnp.zeros_like(acc_sc)

    def run(off):
        heads = lambda ref, sl: jnp.stack([ref[0, sl, h * FOX_AUG:(h + 1) * FOX_AUG] for h in range(FOX_HEADS)])
        for qs in range(tq // qs_w):
            qmin, qmax = qs * qs_w, (qs + 1) * qs_w - 1
            units = []
            for ks in range(tk // ks_w):
                kmin = (0 if off is None else off) + ks * ks_w
                if off is None or kmin + ks_w - 1 <= qmin:
                    units.append((ks, kmin, False))
                elif kmin <= qmax:
                    units.append((ks, kmin, True))
            if not units:
                continue
            qsl = slice(qs * qs_w, (qs + 1) * qs_w)
            q3 = heads(q_ref, qsl)
            m, l, acc = m_sc[:, :, qsl], l_sc[:, :, qsl], acc_sc[:, :, qsl]
            for ks, kmin, masked in units:
                ksl = slice(ks * ks_w, (ks + 1) * ks_w)
                sc = lax.dot_general(heads(k_ref, ksl), q3, (((2,), (2,)), ((0,), (0,))),
                                     preferred_element_type=F32)
                if masked:
                    kpos = kmin + lax.broadcasted_iota(jnp.int32, (ks_w, qs_w), 0)
                    qpos = qmin + lax.broadcasted_iota(jnp.int32, (ks_w, qs_w), 1)
                    sc = jnp.where((kpos > qpos)[None], NEG_BIG, sc)
                m_new = jnp.maximum(m, jnp.max(sc, axis=1, keepdims=True))
                alpha = jnp.exp(m - m_new)
                p = jnp.exp(sc - m_new)
                l = alpha * l + jnp.sum(p, axis=1, keepdims=True)
                vt3 = vt_ref[0, :, ksl].reshape(FOX_HEADS, HEAD_DIM, ks_w)
                acc = alpha * acc + lax.dot_general(vt3, p.astype(BF16), (((2,), (1,)), ((0,), (0,))),
                                                    preferred_element_type=F32)
                m = m_new
            m_sc[:, :, qsl], l_sc[:, :, qsl], acc_sc[:, :, qsl] = m, l, acc

    delta = ki * tk - qi * tq

    @pl.when(delta + tk - 1 <= 0)
    def _():
        run(None)

    for off in range(0, tq, tk):
        @pl.when(delta == off)
        def _(off=off):
            run(off)

    @pl.when(ki == last_k)
    def _():
        for hp in range(FOX_HEADS // 2):
            o_t = jnp.concatenate([acc_sc[2 * hp] / l_sc[2 * hp], acc_sc[2 * hp + 1] / l_sc[2 * hp + 1]], axis=0)
            o_ref[0, :, hp * 2 * HEAD_DIM:(hp + 1) * 2 * HEAD_DIM] = o_t.T.astype(o_ref.dtype)


def _fox(c_fox, vec, bf, ones64, *, tp=256, tq=512, tk=256, qs_w=256, ks_w=128):
    b, l, _ = c_fox.shape
    tp, tq, tk = min(tp, l), min(tq, l), min(tk, l)
    full = lambda shape: pl.BlockSpec(shape, lambda bi, li: (0,) * len(shape))
    q_aug, k_aug, v_t = pl.pallas_call(
        functools.partial(_fox_prep_kernel, tp=tp),
        grid=(b, l // tp),
        in_specs=[
            pl.BlockSpec((1, tp, FOX_PACKED), lambda bi, li: (bi, li, 0)),
            full((8, 512)), full((8, 128)), full((512, 512)),
        ],
        out_specs=[
            pl.BlockSpec((1, tp, FOX_HEADS * FOX_AUG), lambda bi, li: (bi, li, 0)),
            pl.BlockSpec((1, tp, FOX_HEADS * FOX_AUG), lambda bi, li: (bi, li, 0)),
            pl.BlockSpec((1, 512, tp), lambda bi, li: (bi, 0, li)),
        ],
        out_shape=[
            jax.ShapeDtypeStruct((b, l, FOX_HEADS * FOX_AUG), BF16),
            jax.ShapeDtypeStruct((b, l, FOX_HEADS * FOX_AUG), BF16),
            jax.ShapeDtypeStruct((b, 512, l), BF16),
        ],
        scratch_shapes=[pltpu.VMEM((1, 128), F32)],
        compiler_params=_cparams(("parallel", "arbitrary")),
        name="fox_prep",
    )(c_fox, vec, bf, ones64)

    pairs = [(qi, ki) for qi in range(l // tq) for ki in range(((qi + 1) * tq - 1) // tk + 1)]
    qi_tab = jnp.asarray([p[0] for p in pairs], jnp.int32)
    ki_tab = jnp.asarray([p[1] for p in pairs], jnp.int32)
    grid_spec = pltpu.PrefetchScalarGridSpec(
        num_scalar_prefetch=2,
        grid=(b, len(pairs)),
        in_specs=[
            pl.BlockSpec((1, tq, FOX_HEADS * FOX_AUG), lambda bi, s, qt, kt: (bi, qt[s], 0)),
            pl.BlockSpec((1, tk, FOX_HEADS * FOX_AUG), lambda bi, s, qt, kt: (bi, kt[s], 0)),
            pl.BlockSpec((1, 512, tk), lambda bi, s, qt, kt: (bi, 0, kt[s])),
        ],
        out_specs=pl.BlockSpec((1, tq, 512), lambda bi, s, qt, kt: (bi, qt[s], 0)),
        scratch_shapes=[
            pltpu.VMEM((FOX_HEADS, 1, tq), F32),
            pltpu.VMEM((FOX_HEADS, 1, tq), F32),
            pltpu.VMEM((FOX_HEADS, HEAD_DIM, tq), F32),
        ],
    )
    return pl.pallas_call(
        functools.partial(_fox_attn_kernel, tq=tq, tk=tk, qs_w=min(qs_w, tq), ks_w=min(ks_w, tk)),
        grid_spec=grid_spec,
        out_shape=jax.ShapeDtypeStruct((b, l, 512), BF16),
        compiler_params=_cparams(("parallel", "arbitrary")),
        name="fox_attention",
    )(qi_tab, ki_tab, q_aug, k_aug, v_t)


def _fox_params(b_f, q_g, k_g):
    vec = jnp.zeros((8, 512), F32).at[0].set(jnp.tile(q_g, FOX_HEADS)).at[1].set(jnp.tile(k_g, FOX_HEADS))
    bf = jnp.zeros((8, 128), F32).at[0, :FOX_HEADS].set(b_f)
    return vec, bf, _block_ones(512, HEAD_DIM)


def _gelu_tanh(x):
    return 0.5 * x * (1.0 + jnp.tanh(math.sqrt(2.0 / math.pi) * (x + 0.044715 * (x * x * x))))


def _s5_core_kernel(u_ref, mt_ref, bend_ref, cpow_ref, apow_ref, d_ref, y_ref, *, nb, levels):
    u = u_ref[0]
    rows = u.shape[0]
    ub = u.astype(BF16)
    y = jnp.dot(ub, mt_ref[0], preferred_element_type=F32)
    x = jnp.dot(ub, bend_ref[0], preferred_element_type=F32)
    blk = lax.broadcasted_iota(jnp.int32, (rows, 1), 0) % nb
    lane = lax.broadcasted_iota(jnp.int32, (1, 2 * S5_STATE), 1)
    for lv in range(levels):
        sh = 1 << lv
        ap = apow_ref[0, lv:lv + 1, :]
        ap_sw = pltpu.roll(ap, S5_STATE, axis=1)
        c1 = jnp.where(lane < S5_STATE, ap, ap_sw)
        c2 = jnp.where(lane < S5_STATE, -ap_sw, ap)
        xs = jnp.where(blk >= sh, pltpu.roll(x, sh, axis=0), 0.0)
        x = x + c1 * xs + c2 * pltpu.roll(xs, S5_STATE, axis=1)
    x_in = jnp.where(blk >= 1, pltpu.roll(x, 1, axis=0), 0.0)
    y = y + jnp.dot(x_in.astype(BF16), cpow_ref[0], preferred_element_type=F32)
    y_ref[0] = (y + u.astype(F32) * d_ref[0]).astype(y_ref.dtype)


def _s5_glu_kernel(y_ref, w_ref, b_ref, o_ref):
    z = _gelu_tanh(y_ref[...].astype(F32))
    o_ref[...] = (z * _sigmoid(_bdot(z, w_ref[...]) + b_ref[0:1, :])).astype(o_ref.dtype)


def _s5(u, mt, bend, cpow, apow, d_row, w_glu, b_glu):
    b, l, _ = u.shape
    nb = l // S5_BLOCK
    rows = b * nb
    levels = max(1, (nb - 1).bit_length())
    ut = u.astype(BF16).reshape(b, nb, S5_BLOCK, S5_GROUPS, S5_GROUP).transpose(3, 0, 1, 2, 4)
    ut = ut.reshape(S5_GROUPS, rows, S5_ROW)
    per_g = lambda shape: pl.BlockSpec((1,) + shape, lambda g: (g, 0, 0))
    yt = pl.pallas_call(
        functools.partial(_s5_core_kernel, nb=nb, levels=levels),
        grid=(S5_GROUPS,),
        in_specs=[per_g((rows, S5_ROW)), per_g((S5_ROW, S5_ROW)), per_g((S5_ROW, 2 * S5_STATE)),
                  per_g((2 * S5_STATE, S5_ROW)), per_g((8, 2 * S5_STATE)), per_g((1, S5_ROW))],
        out_specs=per_g((rows, S5_ROW)),
        out_shape=jax.ShapeDtypeStruct((S5_GROUPS, rows, S5_ROW), BF16),
        compiler_params=_cparams(("parallel",)),
        name="s5_core",
    )(ut, mt, bend, cpow, apow, d_row)
    y = yt.reshape(S5_GROUPS, b, nb, S5_BLOCK, S5_GROUP).transpose(1, 2, 3, 0, 4).reshape(b * l, 512)
    tm = min(512, b * l)
    out = pl.pallas_call(
        _s5_glu_kernel,
        grid=(b * l // tm,),
        in_specs=[pl.BlockSpec((tm, 512), lambda i: (i, 0)),
                  pl.BlockSpec((512, 512), lambda i: (0, 0)),
                  pl.BlockSpec((8, 512), lambda i: (0, 0))],
        out_specs=pl.BlockSpec((tm, 512), lambda i: (i, 0)),
        out_shape=jax.ShapeDtypeStruct((b * l, 512), BF16),
        compiler_params=_cparams(("parallel",)),
        name="s5_glu",
    )(y, w_glu, b_glu)
    return out.reshape(b, l, 512)


def _s5_params(lam_re, lam_im, log_dt, b_re, b_im, c_re, c_im, d_skip, w_glu, b_glu):
    t = S5_BLOCK
    lr, li = lam_re.astype(F32), lam_im.astype(F32)
    dt = jnp.exp(log_dt.astype(F32))[:, None]
    mag = jnp.exp(lr * dt)
    ar, ai = mag * jnp.cos(li * dt), mag * jnp.sin(li * dt)
    den = lr * lr + li * li
    zr = ((ar - 1.0) * lr + ai * li) / den
    zi = (ai * lr - (ar - 1.0) * li) / den
    br, bi = b_re.astype(F32), b_im.astype(F32)
    bbr = zr[..., None] * br - zi[..., None] * bi
    bbi = zr[..., None] * bi + zi[..., None] * br
    pr, pi = [jnp.ones_like(ar)], [jnp.zeros_like(ar)]
    for _ in range(t):
        pr, pi = pr + [pr[-1] * ar - pi[-1] * ai], pi + [pr[-1] * ai + pi[-1] * ar]
    pr, pi = jnp.stack(pr), jnp.stack(pi)
    cr, ci = c_re.astype(F32), c_im.astype(F32)
    vr = pr[..., None] * bbr[None] - pi[..., None] * bbi[None]
    vi = pr[..., None] * bbi[None] + pi[..., None] * bbr[None]
    kern = jnp.einsum('ghp,tgpj->tghj', cr, vr[:t]) - jnp.einsum('ghp,tgpj->tghj', ci, vi[:t])
    sidx = jnp.arange(t)
    tau = sidx[None, :] - sidx[:, None]
    blocks = jnp.where((tau >= 0)[..., None, None, None], kern[jnp.clip(tau, 0, t - 1)], 0.0)
    mt = blocks.transpose(2, 0, 4, 1, 3).reshape(S5_GROUPS, S5_ROW, S5_ROW)
    vend_r = vr[t - 1 - sidx]
    vend_i = vi[t - 1 - sidx]
    bend = jnp.concatenate([vend_r.transpose(1, 0, 3, 2).reshape(S5_GROUPS, S5_ROW, S5_STATE),
                            vend_i.transpose(1, 0, 3, 2).reshape(S5_GROUPS, S5_ROW, S5_STATE)], axis=-1)
    p1r, p1i = pr[1:], pi[1:]
    cre = cr[None] * p1r[:, :, None, :] - ci[None] * p1i[:, :, None, :]
    cim = -cr[None] * p1i[:, :, None, :] - ci[None] * p1r[:, :, None, :]
    cpow = jnp.concatenate([cre.transpose(1, 3, 0, 2).reshape(S5_GROUPS, S5_STATE, S5_ROW),
                            cim.transpose(1, 3, 0, 2).reshape(S5_GROUPS, S5_STATE, S5_ROW)], axis=1)
    qr, qi, rows = pr[t], pi[t], []
    for _ in range(8):
        rows.append(jnp.concatenate([qr, qi], axis=-1))
        qr, qi = qr * qr - qi * qi, 2.0 * qr * qi
    apow = jnp.stack(rows, axis=1)
    d_row = jnp.tile(d_skip.astype(F32), (1, t)).reshape(S5_GROUPS, 1, S5_ROW)
    b_pad = jnp.zeros((8, 512), F32).at[0].set(b_glu.astype(F32))
    return (mt.astype(BF16), bend.astype(BF16), cpow.astype(BF16), apow, d_row,
            w_glu.astype(BF16), b_pad)


def _rmsnorm_bf16(x, gain):
    ms = jnp.mean(x * x, axis=-1, keepdims=True)
    return (x * lax.rsqrt(ms + NORM_EPS) * gain).astype(BF16)


def _ffn_kernel(x_ref, g_ref, wg_ref, wu_ref, wo_ref, o_ref, h_sc, acc_sc):
    f = pl.program_id(1)

    @pl.when(f == 0)
    def _():
        h_sc[...] = _rmsnorm_bf16(x_ref[...], g_ref[0:1, :])
        acc_sc[...] = jnp.zeros_like(acc_sc)

    h = h_sc[...]
    gate = jnp.dot(h, wg_ref[...], preferred_element_type=F32)
    up = jnp.dot(h, wu_ref[...], preferred_element_type=F32)
    act = (_silu(gate) * up).astype(BF16)
    acc_sc[...] += jnp.dot(act, wo_ref[...], preferred_element_type=F32)

    @pl.when(f == pl.num_programs(1) - 1)
    def _():
        o_ref[...] = x_ref[...] + FFN_RES * acc_sc[...]


def _ffn(x, gain, w_in, w_out, layer, *, tm=512, tf=512):
    m, d = x.shape
    d_ff = w_out.shape[1]
    tm = min(tm, m)
    nf = d_ff // tf
    return pl.pallas_call(
        _ffn_kernel,
        grid=(m // tm, nf),
        in_specs=[
            pl.BlockSpec((tm, d), lambda i, f: (i, 0)),
            pl.BlockSpec((8, d), lambda i, f: (0, 0)),
            pl.BlockSpec((None, d, tf), lambda i, f: (layer, 0, f)),
            pl.BlockSpec((None, d, tf), lambda i, f: (layer, 0, nf + f)),
            pl.BlockSpec((None, tf, d), lambda i, f: (layer, f, 0)),
        ],
        out_specs=pl.BlockSpec((tm, d), lambda i, f: (i, 0)),
        out_shape=jax.ShapeDtypeStruct((m, d), F32),
        scratch_shapes=[pltpu.VMEM((tm, d), BF16), pltpu.VMEM((tm, d), F32)],
        compiler_params=_cparams(("parallel", "arbitrary")),
        name="ffn",
    )(x, gain, w_in, w_in, w_out)


def _proj_kernel(x_ref, g_ref, w_rw_ref, w_gla_ref, w_fox_ref, w_s5_ref, rw_ref, gla_ref, fox_ref, s5_ref):
    h = _rmsnorm_bf16(x_ref[...], g_ref[0:1, :])
    rw_ref[...] = jnp.dot(h, w_rw_ref[...], preferred_element_type=F32)
    gla_ref[...] = jnp.dot(h, w_gla_ref[...], preferred_element_type=F32)
    fox_ref[...] = jnp.dot(h, w_fox_ref[...], preferred_element_type=F32)
    s5_ref[...] = jnp.dot(h, w_s5_ref[...], preferred_element_type=F32).astype(s5_ref.dtype)


def _proj(x, gain, w_rw, w_gla, w_fox, w_s5, *, tm=256):
    m, d = x.shape
    tm = min(tm, m)
    widths = (RW_COLS, GLA_PACKED, FOX_PACKED, 512)
    resident = lambda n: pl.BlockSpec((d, n), lambda i: (0, 0), pipeline_mode=pl.Buffered(1))
    return pl.pallas_call(
        _proj_kernel,
        grid=(m // tm,),
        in_specs=[pl.BlockSpec((tm, d), lambda i: (i, 0)), pl.BlockSpec((8, d), lambda i: (0, 0))]
                 + [resident(n) for n in widths],
        out_specs=[pl.BlockSpec((tm, n), lambda i: (i, 0)) for n in widths],
        out_shape=[jax.ShapeDtypeStruct((m, n), F32) for n in widths[:3]]
                  + [jax.ShapeDtypeStruct((m, widths[3]), BF16)],
        compiler_params=_cparams(("parallel",)),
        name="mixer_in_proj",
    )(x, gain, w_rw, w_gla, w_fox, w_s5)


def _merge_kernel(x_ref, g_ref, y0_ref, y1_ref, y2_ref, y3_ref, wg_ref, wb_ref, wo_ref, o_ref, h_sc, acc_sc):
    n = pl.program_id(1)

    @pl.when(n == 0)
    def _():
        h_sc[...] = _rmsnorm_bf16(x_ref[...], g_ref[0:1, :])
        acc_sc[...] = jnp.zeros_like(acc_sc)

    h = h_sc[...]
    merged = None
    for j, y_ref in enumerate((y0_ref, y1_ref, y2_ref, y3_ref)):
        gate = _sigmoid(jnp.dot(h, wg_ref[j], preferred_element_type=F32))
        term = gate * jnp.dot(y_ref[...], wb_ref[j], preferred_element_type=F32)
        merged = term if merged is None else merged + term
    acc_sc[...] += jnp.dot(merged.astype(BF16), wo_ref[...], preferred_element_type=F32)

    @pl.when(n == pl.num_programs(1) - 1)
    def _():
        o_ref[...] = x_ref[...] + acc_sc[...]


def _merge(x, gain, ys, w_gate, w_branch, w_o, layer, *, tm=512, tn=256):
    m, d = x.shape
    tm = min(tm, m)
    bw = ys[0].shape[1]
    return pl.pallas_call(
        _merge_kernel,
        grid=(m // tm, d // tn),
        in_specs=[pl.BlockSpec((tm, d), lambda i, n: (i, 0)), pl.BlockSpec((8, d), lambda i, n: (0, 0))]
                 + [pl.BlockSpec((tm, bw), lambda i, n: (i, 0)) for _ in ys]
                 + [pl.BlockSpec((None, 4, d, tn), lambda i, n: (layer, 0, 0, n)),
                    pl.BlockSpec((None, 4, bw, tn), lambda i, n: (layer, 0, 0, n)),
                    pl.BlockSpec((None, tn, d), lambda i, n: (layer, n, 0))],
        out_specs=pl.BlockSpec((tm, d), lambda i, n: (i, 0)),
        out_shape=jax.ShapeDtypeStruct((m, d), F32),
        scratch_shapes=[pltpu.VMEM((tm, d), BF16), pltpu.VMEM((tm, d), F32)],
        compiler_params=_cparams(("parallel", "arbitrary")),
        name="merge",
    )(x, gain, *ys, w_gate, w_branch, w_o)


def _gain_rows(g):
    return jnp.broadcast_to(g.astype(F32)[None, :], (8, g.shape[0]))


def _pack_w_in(w):
    d = w.shape[0]
    o = 0
    w_rw = w[:, o:o + RW_COLS]
    o += RW_COLS
    q, k, v, glr, og = (w[:, o:o + 256], w[:, o + 256:o + 512], w[:, o + 512:o + 1024],
                        w[:, o + 1024:o + 1040], w[:, o + 1040:o + 1552])
    w_gla = jnp.concatenate([q, k, v, og, glr, jnp.zeros((d, 128 - GLA_GATE_RANK), w.dtype)], axis=1)
    o += 1552
    w_fox = jnp.concatenate([w[:, o:o + 1544], jnp.zeros((d, 128 - FOX_HEADS), w.dtype)], axis=1)
    o += 1544
    w_s5 = w[:, o:o + 512]
    return [t.astype(BF16) for t in (w_rw, w_gla, w_fox, w_s5)]


def kernel(x, norm_ffa, w_ffa_in, w_ffa_out, norm_mix, w_in, rw_mu, rw_w0, rw_w2, rw_a0, rw_a2, rw_g2, rw_kk, rw_ka, rw_rk, rw_ln_g, rw_ln_b, gla_wg2, gla_bg2, gla_norm_g, fox_bf, fox_qg, fox_kg, s5_lam_re, s5_lam_im, s5_log_dt, s5_b_re, s5_b_im, s5_c_re, s5_c_im, s5_d, s5_w_glu, s5_b_glu, w_gate, w_branch, w_o, norm_ffb, w_ffb_in, w_ffb_out):
    b, l, d = x.shape
    m = b * l
    x = x.reshape(m, d).astype(F32)
    w_ffa_in, w_ffa_out, w_ffb_in, w_ffb_out, w_gate, w_branch, w_o = (
        w.astype(BF16) for w in (w_ffa_in, w_ffa_out, w_ffb_in, w_ffb_out, w_gate, w_branch, w_o))
    for i in range(norm_ffa.shape[0]):
        x = _ffn(x, _gain_rows(norm_ffa[i]), w_ffa_in, w_ffa_out, i)
        gain_mix = _gain_rows(norm_mix[i])
        c_rw, c_gla, c_fox, c_s5 = _proj(x, gain_mix, *_pack_w_in(w_in[i]))
        y_rw = _rwkv7(c_rw.reshape(b, l, -1),
                      *_rwkv7_params(rw_mu[i], rw_w0[i], rw_w2[i], rw_a0[i], rw_a2[i], rw_g2[i],
                                     rw_kk[i], rw_ka[i], rw_rk[i], rw_ln_g[i], rw_ln_b[i]))
        y_gla = _gla(c_gla.reshape(b, l, -1), *_gla_params(gla_wg2[i], gla_bg2[i], gla_norm_g[i]))
        y_fox = _fox(c_fox.reshape(b, l, -1), *_fox_params(fox_bf[i], fox_qg[i], fox_kg[i]))
        y_s5 = _s5(c_s5.reshape(b, l, -1),
                   *_s5_params(s5_lam_re[i], s5_lam_im[i], s5_log_dt[i], s5_b_re[i], s5_b_im[i],
                               s5_c_re[i], s5_c_im[i], s5_d[i], s5_w_glu[i], s5_b_glu[i]))
        ys = [y.reshape(m, -1) for y in (y_rw, y_gla, y_fox, y_s5)]
        x = _merge(x, gain_mix, ys, w_gate, w_branch, w_o, i)
        x = _ffn(x, _gain_rows(norm_ffb[i]), w_ffb_in, w_ffb_out, i)
    return x.reshape(b, l, d)
```

```python
import functools
import math

import jax
import jax.numpy as jnp
from jax import lax
from jax.experimental import pallas as pl
from jax.experimental.pallas import tpu as pltpu

F32 = jnp.float32
BF16 = jnp.bfloat16

D_MODEL = 2048
D_FF = 5632
FFN_RES = 0.5
NORM_EPS = 1e-6
HEAD_DIM = 64
CHUNK = 64

RW_HEADS = 8
RW_WIDTH = 512
RW_COLS = 1792
RW_GN_EPS = 64e-5

GLA_HEADS = 4
GLA_DK = 64
GLA_DV = 128
GLA_GATE_RANK = 16
GLA_GATE_NORM = 16.0
GLA_PACKED = 1664

FOX_HEADS = 8
FOX_PACKED = 1664
FOX_AUG = 128

S5_GROUPS = 32
S5_GROUP = 16
S5_STATE = 64
S5_BLOCK = 16
S5_ROW = S5_BLOCK * S5_GROUP

VMEM_LIMIT = 58 * 1024 * 1024


def _cparams(sem):
    return pltpu.CompilerParams(dimension_semantics=sem, vmem_limit_bytes=VMEM_LIMIT)


def _bdot(a, b):
    return jnp.dot(a.astype(BF16), b.astype(BF16), preferred_element_type=F32)


def _bdot_nt(a, b):
    return lax.dot_general(a.astype(BF16), b.astype(BF16), (((1,), (1,)), ((), ())),
                           preferred_element_type=F32)


def _bdot_tn(a, b):
    return lax.dot_general(a.astype(BF16), b.astype(BF16), (((0,), (0,)), ((), ())),
                           preferred_element_type=F32)


def _split3(x):
    h1 = x.astype(BF16)
    r1 = x - h1.astype(F32)
    h2 = r1.astype(BF16)
    h3 = (r1 - h2.astype(F32)).astype(BF16)
    return h1, h2, h3


def _exact_left_dot(m_bf16, x):
    h1, h2, h3 = _split3(x)
    d = lambda h: jnp.dot(m_bf16, h, preferred_element_type=F32)
    return d(h1) + d(h2) + d(h3)


def _exact_right_dot(x, m_bf16):
    h1, h2, h3 = _split3(x)
    d = lambda h: jnp.dot(h, m_bf16, preferred_element_type=F32)
    return d(h1) + d(h2) + d(h3)


def _seg_sums(xs, ones_bd):
    m = xs[0].shape[0]
    parts = []
    for x in xs:
        hi = x.astype(BF16)
        parts += [hi, (x - hi.astype(F32)).astype(BF16)]
    out = jnp.dot(jnp.concatenate(parts, axis=0), ones_bd, preferred_element_type=F32)
    return [out[2 * i * m:(2 * i + 1) * m] + out[(2 * i + 1) * m:(2 * i + 2) * m] for i in range(len(xs))]


def _tri_masks(n):
    row = lax.broadcasted_iota(jnp.int32, (n, n), 0)
    col = lax.broadcasted_iota(jnp.int32, (n, n), 1)
    return row, col


def _sigmoid(x):
    return 1.0 / (1.0 + jnp.exp(-x))


def _softplus(x):
    return jnp.maximum(x, 0.0) + jnp.log(1.0 + jnp.exp(-jnp.abs(x)))


def _log_sigmoid(x):
    return -_softplus(-x)


def _silu(x):
    return x * _sigmoid(x)


def _bmm(a, b):
    return lax.dot_general(a.astype(BF16), b.astype(BF16), (((2,), (1,)), ((0,), (0,))),
                           preferred_element_type=F32)


def _bmm_nt(a, b):
    return lax.dot_general(a.astype(BF16), b.astype(BF16), (((2,), (2,)), ((0,), (0,))),
                           preferred_element_type=F32)


def _bmm_tn(a, b):
    return lax.dot_general(a.astype(BF16), b.astype(BF16), (((1,), (1,)), ((0,), (0,))),
                           preferred_element_type=F32)


def _unit_lower_inverse(a, row, col):
    eye = jnp.where(row == col, 1.0, 0.0).astype(F32)[None]
    ad = jnp.where(((row // 8) == (col // 8))[None], a, 0.0)
    x = eye + ad
    a2 = _bmm(ad, ad)
    x = x + _bmm(a2, x)
    a4 = _bmm(a2, a2)
    x = x + _bmm(a4, x)
    s = 8
    while s < CHUNK:
        off = ((row // (2 * s)) == (col // (2 * s))) & ((row // s) != (col // s))
        e = jnp.where(off[None], a, 0.0)
        x = x + _bmm(x, _bmm(e, x))
        s *= 2
    return x


def _rwkv7_kernel(c_ref, mu_ref, vec_ref, w2_ref, a2_ref, g2_ref, ones_ref, o_ref,
                  prev_sc, state_sc, y_sc, *, tl):
    li = pl.program_id(1)
    nc = tl // CHUNK
    nh = RW_HEADS

    @pl.when(li == 0)
    def _():
        prev_sc[...] = jnp.zeros_like(prev_sc)
        state_sc[...] = jnp.zeros_like(state_sc)

    c = c_ref[0]
    rows = lax.broadcasted_iota(jnp.int32, (tl, 1), 0)
    c_prev = jnp.where(rows == 0, prev_sc[...], pltpu.roll(c, 1, axis=0))
    prev_sc[...] = c[tl - 1:tl, :]
    c = c + (c_prev - c) * mu_ref[...]

    w0, a0, k_k, k_a = vec_ref[0:1, :], vec_ref[1:2, :], vec_ref[2:3, :], vec_ref[3:4, :]
    r_k, ln_g, ln_b = vec_ref[4:5, :], vec_ref[5:6, :], vec_ref[6:7, :]
    ones_bd = ones_ref[...]

    r = c[:, 0:512]
    k = c[:, 512:1024]
    v = c[:, 1024:1536]
    xw = c[:, 1536:1600]
    xa = c[:, 1600:1664]
    xg = c[:, 1664:1792]

    w_log = -_softplus(-(w0 + _bdot(jnp.tanh(xw), w2_ref[...]))) - 0.5
    logw = -jnp.exp(w_log)
    a = _sigmoid(a0 + _bdot(xa, a2_ref[...]))
    g = _bdot(_sigmoid(xg), g2_ref[...])

    kk = k * k_k
    k = k * (1.0 + (a - 1.0) * k_a)
    kk_ss, bonus = _seg_sums([kk * kk, r * k * r_k], ones_bd)
    kk = kk / jnp.maximum(jnp.sqrt(kk_ss), 1e-12)
    av = -kk
    bv = kk * a

    rowt, colt = _tri_masks(tl)
    same_chunk = (rowt // CHUNK) == (colt // CHUNK)
    cum = _exact_left_dot(jnp.where(same_chunk & (rowt >= colt), 1.0, 0.0).astype(BF16), logw)
    tot = jnp.concatenate([jnp.broadcast_to(cum[(j + 1) * CHUNK - 1:(j + 1) * CHUNK], (CHUNK, RW_WIDTH))
                           for j in range(nc)], axis=0)
    e_neg = jnp.exp(-cum)
    e_end = jnp.exp(tot - cum)
    p_c = jnp.exp(tot)

    def heads(x, nrows=CHUNK):
        return jnp.stack([x[j * CHUNK:j * CHUNK + nrows, h * HEAD_DIM:(h + 1) * HEAD_DIM]
                          for j in range(nc) for h in range(nh)])

    a3 = heads(av * jnp.exp(cum - logw))
    r3 = heads(r * jnp.exp(cum))
    v3 = heads(v)
    lhs3 = jnp.concatenate([a3, r3], axis=1)
    rhs3 = jnp.concatenate([heads(bv * e_neg), heads(k * e_neg)], axis=1)
    ber3 = jnp.concatenate([heads(bv * e_end), heads(k * e_end)], axis=1)
    p3 = heads(p_c, 1)

    row, col = _tri_masks(CHUNK)
    strict = (row > col)[None]
    incl = (row >= col)[None]
    aall = _bmm_nt(lhs3, rhs3)
    a_ab = jnp.where(strict, aall[:, :CHUNK, :CHUNK], 0.0)
    a_ak = jnp.where(strict, aall[:, :CHUNK, CHUNK:], 0.0)
    arbk = jnp.concatenate([jnp.where(incl, aall[:, CHUNK:, :CHUNK], 0.0),
                            jnp.where(incl, aall[:, CHUNK:, CHUNK:], 0.0)], axis=2)
    tinv = _unit_lower_inverse(a_ab, row, col)
    wu0 = _bmm(tinv, jnp.concatenate([a3, _bmm(a_ak, v3)], axis=2))
    wr3 = jnp.concatenate([wu0[:, :, :HEAD_DIM], r3], axis=1)
    u03 = wu0[:, :, HEAD_DIM:]

    s = state_sc[...]
    for j in range(nc):
        sel = slice(j * nh, (j + 1) * nh)
        proj = _bmm_nt(wr3[sel], s)
        uv = jnp.concatenate([proj[:, :CHUNK] + u03[sel], v3[sel]], axis=1)
        y3 = proj[:, CHUNK:] + _bmm(arbk[sel], uv)
        s = s * p3[sel] + _bmm_tn(uv, ber3[sel])
        for h in range(nh):
            y_sc[j * CHUNK:(j + 1) * CHUNK, h * HEAD_DIM:(h + 1) * HEAD_DIM] = y3[h]
    state_sc[...] = s

    y = y_sc[...]
    yc = y - _seg_sums([y], ones_bd)[0] * (1.0 / HEAD_DIM)
    var = _seg_sums([yc * yc], ones_bd)[0] * (1.0 / HEAD_DIM)
    y = yc * lax.rsqrt(var + RW_GN_EPS) * ln_g + ln_b
    y = y + bonus * v
    o_ref[0] = (y * g).astype(o_ref.dtype)


def _rwkv7(c_rw, mu, vec, w2, a2, g2, ones64, *, tl=4 * CHUNK):
    b, l, _ = c_rw.shape
    full = lambda shape: pl.BlockSpec(shape, lambda bi, li: (0,) * len(shape))
    return pl.pallas_call(
        functools.partial(_rwkv7_kernel, tl=tl),
        grid=(b, l // tl),
        in_specs=[
            pl.BlockSpec((1, tl, RW_COLS), lambda bi, li: (bi, li, 0)),
            full((1, RW_COLS)), full((8, RW_WIDTH)), full((64, RW_WIDTH)), full((64, RW_WIDTH)),
            full((128, RW_WIDTH)), full((RW_WIDTH, RW_WIDTH)),
        ],
        out_specs=pl.BlockSpec((1, tl, RW_WIDTH), lambda bi, li: (bi, li, 0)),
        out_shape=jax.ShapeDtypeStruct((b, l, RW_WIDTH), BF16),
        scratch_shapes=[
            pltpu.VMEM((1, RW_COLS), F32),
            pltpu.VMEM((RW_HEADS, HEAD_DIM, HEAD_DIM), F32),
            pltpu.VMEM((tl, RW_WIDTH), F32),
        ],
        compiler_params=_cparams(("parallel", "arbitrary")),
        name="rwkv7_mixer",
    )(c_rw, mu, vec, w2, a2, g2, ones64)


def _block_ones(width, block):
    i = jnp.arange(width) // block
    return (i[:, None] == i[None, :]).astype(BF16)


def _rwkv7_params(mu, w0, w2, a0, a2, g2, k_k, k_a, r_k, ln_g, ln_b):
    vec = jnp.stack([w0, a0, k_k, k_a, r_k.reshape(-1), ln_g, ln_b, jnp.zeros_like(w0)]).astype(F32)
    return (mu.reshape(1, -1).astype(F32), vec, w2.astype(BF16), a2.astype(BF16), g2.astype(BF16),
            _block_ones(RW_WIDTH, HEAD_DIM))


def _gla_kernel(c_ref, wg_ref, vec_ref, o_ref, state_sc, o_sc, *, tl):
    li = pl.program_id(1)

    @pl.when(li == 0)
    def _():
        state_sc[...] = jnp.zeros_like(state_sc)

    c = c_ref[0]
    q = c[:, 0:256] * (GLA_DK ** -0.5)
    k = c[:, 256:512]
    v = c[:, 512:1024]
    og = c[:, 1024:1536]
    glr = c[:, 1536:1664]
    b_g2 = vec_ref[0:1, 0:256]
    norm_g = vec_ref[1:2, :]
    log_a = _log_sigmoid(_bdot(glr, wg_ref[...]) + b_g2) * (1.0 / GLA_GATE_NORM)

    nc = tl // CHUNK
    nh = GLA_HEADS
    rowt, colt = _tri_masks(tl)
    same_chunk = (rowt // CHUNK) == (colt // CHUNK)
    cum = _exact_left_dot(jnp.where(same_chunk & (rowt >= colt), 1.0, 0.0).astype(BF16), log_a)
    tot = jnp.concatenate([jnp.broadcast_to(cum[(j + 1) * CHUNK - 1:(j + 1) * CHUNK], (CHUNK, 256))
                           for j in range(nc)], axis=0)

    def heads(x, width, nrows=CHUNK):
        return jnp.stack([x[j * CHUNK:j * CHUNK + nrows, h * width:(h + 1) * width]
                          for j in range(nc) for h in range(nh)])

    qd3 = heads(q * jnp.exp(cum), GLA_DK)
    v3 = heads(v, GLA_DV)
    row, col = _tri_masks(CHUNK)
    att = jnp.where((row >= col)[None], _bmm_nt(qd3, heads(k * jnp.exp(-cum), GLA_DK)), 0.0)
    o3 = _bmm(att, v3)
    kv3 = _bmm_tn(v3, heads(k * jnp.exp(tot - cum), GLA_DK))
    dec3 = heads(jnp.exp(tot), GLA_DK, 1)

    s = state_sc[...]
    for j in range(nc):
        sel = slice(j * nh, (j + 1) * nh)
        oj = o3[sel] + _bmm_nt(qd3[sel], s)
        s = s * dec3[sel] + kv3[sel]
        for h in range(nh):
            o_sc[j * CHUNK:(j + 1) * CHUNK, h * GLA_DV:(h + 1) * GLA_DV] = oj[h]
    state_sc[...] = s

    o = o_sc[...]
    outs = []
    for h in range(GLA_HEADS):
        oh = o[:, h * GLA_DV:(h + 1) * GLA_DV]
        ms = jnp.mean(oh * oh, axis=-1, keepdims=True)
        outs.append(oh * lax.rsqrt(ms + NORM_EPS) * norm_g[:, 0:GLA_DV])
    o = jnp.concatenate(outs, axis=-1)
    o_ref[0] = (o * _silu(og)).astype(o_ref.dtype)


def _gla(c_gla, wg, vec, *, tl=4 * CHUNK):
    b, l, _ = c_gla.shape
    full = lambda shape: pl.BlockSpec(shape, lambda bi, li: (0,) * len(shape))
    return pl.pallas_call(
        functools.partial(_gla_kernel, tl=tl),
        grid=(b, l // tl),
        in_specs=[
            pl.BlockSpec((1, tl, GLA_PACKED), lambda bi, li: (bi, li, 0)),
            full((128, 256)), full((8, 512)),
        ],
        out_specs=pl.BlockSpec((1, tl, 512), lambda bi, li: (bi, li, 0)),
        out_shape=jax.ShapeDtypeStruct((b, l, 512), BF16),
        scratch_shapes=[
            pltpu.VMEM((GLA_HEADS, GLA_DV, GLA_DK), F32),
            pltpu.VMEM((tl, 512), F32),
        ],
        compiler_params=_cparams(("parallel", "arbitrary")),
        name="gla_mixer",
    )(c_gla, wg, vec)


def _gla_params(w_g2, b_g2, norm_g):
    wg = jnp.zeros((128, 256), F32).at[:GLA_GATE_RANK].set(w_g2).astype(BF16)
    vec = jnp.zeros((8, 512), F32).at[0, :256].set(b_g2).at[1, :GLA_DV].set(norm_g)
    return wg, vec


NEG_BIG = -1e30


def _fox_prep_kernel(c_ref, vec_ref, bf_ref, ones_ref, q_ref, k_ref, v_ref, carry_sc, *, tp):
    li = pl.program_id(1)

    @pl.when(li == 0)
    def _():
        carry_sc[...] = jnp.zeros_like(carry_sc)

    c = c_ref[0]
    ones_bd = ones_ref[...]
    q = c[:, 0:512]
    k = c[:, 512:1024]
    v_ref[0] = c[:, 1024:1536].T.astype(v_ref.dtype)
    q_ss, k_ss = _seg_sums([q * q, k * k], ones_bd)
    qn = q * lax.rsqrt(q_ss * (1.0 / HEAD_DIM) + NORM_EPS) * vec_ref[0:1, :] * (HEAD_DIM ** -0.5)
    kn = k * lax.rsqrt(k_ss * (1.0 / HEAD_DIM) + NORM_EPS) * vec_ref[1:2, :]

    log_f = _log_sigmoid(c[:, 1536:1664] + bf_ref[0:1, :])
    row, col = _tri_masks(tp)
    tril_incl = jnp.where(row >= col, 1.0, 0.0).astype(BF16)
    cum = _exact_left_dot(tril_incl, log_f) + carry_sc[...]
    carry_sc[...] = cum[tp - 1:tp, :]

    lane = lax.broadcasted_iota(jnp.int32, (tp, HEAD_DIM), 1)
    for h in range(FOX_HEADS):
        f = jnp.broadcast_to(cum[:, h:h + 1], (tp, HEAD_DIM))
        f1 = f.astype(BF16).astype(F32)
        r1 = f - f1
        f2 = r1.astype(BF16).astype(F32)
        f3 = r1 - f2
        fq = jnp.where(lane == 0, f1, jnp.where(lane == 1, f2, jnp.where(lane == 2, f3, 0.0)))
        aug_q = jnp.where(lane < 3, fq, jnp.where(lane < 6, 1.0, 0.0))
        fk = jnp.where(lane == 3, f1, jnp.where(lane == 4, f2, jnp.where(lane == 5, f3, 0.0)))
        aug_k = jnp.where(lane < 3, 1.0, -fk)
        hs = slice(h * HEAD_DIM, (h + 1) * HEAD_DIM)
        q_ref[0, :, h * FOX_AUG:(h + 1) * FOX_AUG] = jnp.concatenate([qn[:, hs], aug_q], axis=-1).astype(q_ref.dtype)
        k_ref[0, :, h * FOX_AUG:(h + 1) * FOX_AUG] = jnp.concatenate([kn[:, hs], aug_k], axis=-1).astype(k_ref.dtype)


def _fox_attn_kernel(qi_ref, ki_ref, q_ref, k_ref, vt_ref, o_ref, m_sc, l_sc, acc_sc, *, tq, tk, qs_w, ks_w):
    s = pl.program_id(1)
    qi = qi_ref[s]
    ki = ki_ref[s]
    last_k = ((qi + 1) * tq - 1) // tk

    @pl.when(ki == 0)
    def _():
        m_sc[...] = jnp.full_like(m_sc, NEG_BIG)
        l_sc[...] = jnp.zeros_like(l_sc)
        acc_sc[...] = jnp.zeros_like(acc_sc)

    def run(off):
        heads = lambda ref, sl: jnp.stack([ref[0, sl, h * FOX_AUG:(h + 1) * FOX_AUG] for h in range(FOX_HEADS)])
        for qs in range(tq // qs_w):
            qmin, qmax = qs * qs_w, (qs + 1) * qs_w - 1
            units = []
            for ks in range(tk // ks_w):
                kmin = (0 if off is None else off) + ks * ks_w
                if off is None or kmin + ks_w - 1 <= qmin:
                    units.append((ks, kmin, False))
                elif kmin <= qmax:
                    units.append((ks, kmin, True))
            if not units:
                continue
            qsl = slice(qs * qs_w, (qs + 1) * qs_w)
            q3 = heads(q_ref, qsl)
            m, l, acc = m_sc[:, :, qsl], l_sc[:, :, qsl], acc_sc[:, :, qsl]
            for ks, kmin, masked in units:
                ksl = slice(ks * ks_w, (ks + 1) * ks_w)
                sc = lax.dot_general(heads(k_ref, ksl), q3, (((2,), (2,)), ((0,), (0,))),
                                     preferred_element_type=F32)
                if masked:
                    kpos = kmin + lax.broadcasted_iota(jnp.int32, (ks_w, qs_w), 0)
                    qpos = qmin + lax.broadcasted_iota(jnp.int32, (ks_w, qs_w), 1)
                    sc = jnp.where((kpos > qpos)[None], NEG_BIG, sc)
                m_new = jnp.maximum(m, jnp.max(sc, axis=1, keepdims=True))
                alpha = jnp.exp(m - m_new)
                p = jnp.exp(sc - m_new)
                l = alpha * l + jnp.sum(p, axis=1, keepdims=True)
                vt3 = vt_ref[0, :, ksl].reshape(FOX_HEADS, HEAD_DIM, ks_w)
                acc = alpha * acc + lax.dot_general(vt3, p.astype(BF16), (((2,), (1,)), ((0,), (0,))),
                                                    preferred_element_type=F32)
                m = m_new
            m_sc[:, :, qsl], l_sc[:, :, qsl], acc_sc[:, :, qsl] = m, l, acc

    delta = ki * tk - qi * tq

    @pl.when(delta + tk - 1 <= 0)
    def _():
        run(None)

    for off in range(0, tq, tk):
        @pl.when(delta == off)
        def _(off=off):
            run(off)

    @pl.when(ki == last_k)
    def _():
        for hp in range(FOX_HEADS // 2):
            o_t = jnp.concatenate([acc_sc[2 * hp] / l_sc[2 * hp], acc_sc[2 * hp + 1] / l_sc[2 * hp + 1]], axis=0)
            o_ref[0, :, hp * 2 * HEAD_DIM:(hp + 1) * 2 * HEAD_DIM] = o_t.T.astype(o_ref.dtype)


def _fox(c_fox, vec, bf, ones64, *, tp=256, tq=512, tk=256, qs_w=256, ks_w=128):
    b, l, _ = c_fox.shape
    tp, tq, tk = min(tp, l), min(tq, l), min(tk, l)
    full = lambda shape: pl.BlockSpec(shape, lambda bi, li: (0,) * len(shape))
    q_aug, k_aug, v_t = pl.pallas_call(
        functools.partial(_fox_prep_kernel, tp=tp),
        grid=(b, l // tp),
        in_specs=[
            pl.BlockSpec((1, tp, FOX_PACKED), lambda bi, li: (bi, li, 0)),
            full((8, 512)), full((8, 128)), full((512, 512)),
        ],
        out_specs=[
            pl.BlockSpec((1, tp, FOX_HEADS * FOX_AUG), lambda bi, li: (bi, li, 0)),
            pl.BlockSpec((1, tp, FOX_HEADS * FOX_AUG), lambda bi, li: (bi, li, 0)),
            pl.BlockSpec((1, 512, tp), lambda bi, li: (bi, 0, li)),
        ],
        out_shape=[
            jax.ShapeDtypeStruct((b, l, FOX_HEADS * FOX_AUG), BF16),
            jax.ShapeDtypeStruct((b, l, FOX_HEADS * FOX_AUG), BF16),
            jax.ShapeDtypeStruct((b, 512, l), BF16),
        ],
        scratch_shapes=[pltpu.VMEM((1, 128), F32)],
        compiler_params=_cparams(("parallel", "arbitrary")),
        name="fox_prep",
    )(c_fox, vec, bf, ones64)

    pairs = [(qi, ki) for qi in range(l // tq) for ki in range(((qi + 1) * tq - 1) // tk + 1)]
    qi_tab = jnp.asarray([p[0] for p in pairs], jnp.int32)
    ki_tab = jnp.asarray([p[1] for p in pairs], jnp.int32)
    grid_spec = pltpu.PrefetchScalarGridSpec(
        num_scalar_prefetch=2,
        grid=(b, len(pairs)),
        in_specs=[
            pl.BlockSpec((1, tq, FOX_HEADS * FOX_AUG), lambda bi, s, qt, kt: (bi, qt[s], 0)),
            pl.BlockSpec((1, tk, FOX_HEADS * FOX_AUG), lambda bi, s, qt, kt: (bi, kt[s], 0)),
            pl.BlockSpec((1, 512, tk), lambda bi, s, qt, kt: (bi, 0, kt[s])),
        ],
        out_specs=pl.BlockSpec((1, tq, 512), lambda bi, s, qt, kt: (bi, qt[s], 0)),
        scratch_shapes=[
            pltpu.VMEM((FOX_HEADS, 1, tq), F32),
            pltpu.VMEM((FOX_HEADS, 1, tq), F32),
            pltpu.VMEM((FOX_HEADS, HEAD_DIM, tq), F32),
        ],
    )
    return pl.pallas_call(
        functools.partial(_fox_attn_kernel, tq=tq, tk=tk, qs_w=min(qs_w, tq), ks_w=min(ks_w, tk)),
        grid_spec=grid_spec,
        out_shape=jax.ShapeDtypeStruct((b, l, 512), BF16),
        compiler_params=_cparams(("parallel", "arbitrary")),
        name="fox_attention",
    )(qi_tab, ki_tab, q_aug, k_aug, v_t)


def _fox_params(b_f, q_g, k_g):
    vec = jnp.zeros((8, 512), F32).at[0].set(jnp.tile(q_g, FOX_HEADS)).at[1].set(jnp.tile(k_g, FOX_HEADS))
    bf = jnp.zeros((8, 128), F32).at[0, :FOX_HEADS].set(b_f)
    return vec, bf, _block_ones(512, HEAD_DIM)


def _gelu_tanh(x):
    return 0.5 * x * (1.0 + jnp.tanh(math.sqrt(2.0 / math.pi) * (x + 0.044715 * (x * x * x))))


def _s5_core_kernel(u_ref, mt_ref, bend_ref, cpow_ref, apow_ref, d_ref, y_ref, *, nb, levels):
    u = u_ref[0]
    rows = u.shape[0]
    ub = u.astype(BF16)
    y = jnp.dot(ub, mt_ref[0], preferred_element_type=F32)
    x = jnp.dot(ub, bend_ref[0], preferred_element_type=F32)
    blk = lax.broadcasted_iota(jnp.int32, (rows, 1), 0) % nb
    lane = lax.broadcasted_iota(jnp.int32, (1, 2 * S5_STATE), 1)
    for lv in range(levels):
        sh = 1 << lv
        ap = apow_ref[0, lv:lv + 1, :]
        ap_sw = pltpu.roll(ap, S5_STATE, axis=1)
        c1 = jnp.where(lane < S5_STATE, ap, ap_sw)
        c2 = jnp.where(lane < S5_STATE, -ap_sw, ap)
        xs = jnp.where(blk >= sh, pltpu.roll(x, sh, axis=0), 0.0)
        x = x + c1 * xs + c2 * pltpu.roll(xs, S5_STATE, axis=1)
    x_in = jnp.where(blk >= 1, pltpu.roll(x, 1, axis=0), 0.0)
    y = y + jnp.dot(x_in.astype(BF16), cpow_ref[0], preferred_element_type=F32)
    y_ref[0] = (y + u.astype(F32) * d_ref[0]).astype(y_ref.dtype)


def _s5_glu_kernel(y_ref, w_ref, b_ref, o_ref):
    z = _gelu_tanh(y_ref[...].astype(F32))
    o_ref[...] = (z * _sigmoid(_bdot(z, w_ref[...]) + b_ref[0:1, :])).astype(o_ref.dtype)


def _s5(u, mt, bend, cpow, apow, d_row, w_glu, b_glu):
    b, l, _ = u.shape
    nb = l // S5_BLOCK
    rows = b * nb
    levels = max(1, (nb - 1).bit_length())
    ut = u.astype(BF16).reshape(b, nb, S5_BLOCK, S5_GROUPS, S5_GROUP).transpose(3, 0, 1, 2, 4)
    ut = ut.reshape(S5_GROUPS, rows, S5_ROW)
    per_g = lambda shape: pl.BlockSpec((1,) + shape, lambda g: (g, 0, 0))
    yt = pl.pallas_call(
        functools.partial(_s5_core_kernel, nb=nb, levels=levels),
        grid=(S5_GROUPS,),
        in_specs=[per_g((rows, S5_ROW)), per_g((S5_ROW, S5_ROW)), per_g((S5_ROW, 2 * S5_STATE)),
                  per_g((2 * S5_STATE, S5_ROW)), per_g((8, 2 * S5_STATE)), per_g((1, S5_ROW))],
        out_specs=per_g((rows, S5_ROW)),
        out_shape=jax.ShapeDtypeStruct((S5_GROUPS, rows, S5_ROW), BF16),
        compiler_params=_cparams(("parallel",)),
        name="s5_core",
    )(ut, mt, bend, cpow, apow, d_row)
    y = yt.reshape(S5_GROUPS, b, nb, S5_BLOCK, S5_GROUP).transpose(1, 2, 3, 0, 4).reshape(b * l, 512)
    tm = min(512, b * l)
    out = pl.pallas_call(
        _s5_glu_kernel,
        grid=(b * l // tm,),
        in_specs=[pl.BlockSpec((tm, 512), lambda i: (i, 0)),
                  pl.BlockSpec((512, 512), lambda i: (0, 0)),
                  pl.BlockSpec((8, 512), lambda i: (0, 0))],
        out_specs=pl.BlockSpec((tm, 512), lambda i: (i, 0)),
        out_shape=jax.ShapeDtypeStruct((b * l, 512), BF16),
        compiler_params=_cparams(("parallel",)),
        name="s5_glu",
    )(y, w_glu, b_glu)
    return out.reshape(b, l, 512)


def _s5_params(lam_re, lam_im, log_dt, b_re, b_im, c_re, c_im, d_skip, w_glu, b_glu):
    t = S5_BLOCK
    lr, li = lam_re.astype(F32), lam_im.astype(F32)
    dt = jnp.exp(log_dt.astype(F32))[:, None]
    mag = jnp.exp(lr * dt)
    ar, ai = mag * jnp.cos(li * dt), mag * jnp.sin(li * dt)
    den = lr * lr + li * li
    zr = ((ar - 1.0) * lr + ai * li) / den
    zi = (ai * lr - (ar - 1.0) * li) / den
    br, bi = b_re.astype(F32), b_im.astype(F32)
    bbr = zr[..., None] * br - zi[..., None] * bi
    bbi = zr[..., None] * bi + zi[..., None] * br
    pr, pi = [jnp.ones_like(ar)], [jnp.zeros_like(ar)]
    for _ in range(t):
        pr, pi = pr + [pr[-1] * ar - pi[-1] * ai], pi + [pr[-1] * ai + pi[-1] * ar]
    pr, pi = jnp.stack(pr), jnp.stack(pi)
    cr, ci = c_re.astype(F32), c_im.astype(F32)
    vr = pr[..., None] * bbr[None] - pi[..., None] * bbi[None]
    vi = pr[..., None] * bbi[None] + pi[..., None] * bbr[None]
    kern = jnp.einsum('ghp,tgpj->tghj', cr, vr[:t]) - jnp.einsum('ghp,tgpj->tghj', ci, vi[:t])
    sidx = jnp.arange(t)
    tau = sidx[None, :] - sidx[:, None]
    blocks = jnp.where((tau >= 0)[..., None, None, None], kern[jnp.clip(tau, 0, t - 1)], 0.0)
    mt = blocks.transpose(2, 0, 4, 1, 3).reshape(S5_GROUPS, S5_ROW, S5_ROW)
    vend_r = vr[t - 1 - sidx]
    vend_i = vi[t - 1 - sidx]
    bend = jnp.concatenate([vend_r.transpose(1, 0, 3, 2).reshape(S5_GROUPS, S5_ROW, S5_STATE),
                            vend_i.transpose(1, 0, 3, 2).reshape(S5_GROUPS, S5_ROW, S5_STATE)], axis=-1)
    p1r, p1i = pr[1:], pi[1:]
    cre = cr[None] * p1r[:, :, None, :] - ci[None] * p1i[:, :, None, :]
    cim = -cr[None] * p1i[:, :, None, :] - ci[None] * p1r[:, :, None, :]
    cpow = jnp.concatenate([cre.transpose(1, 3, 0, 2).reshape(S5_GROUPS, S5_STATE, S5_ROW),
                            cim.transpose(1, 3, 0, 2).reshape(S5_GROUPS, S5_STATE, S5_ROW)], axis=1)
    qr, qi, rows = pr[t], pi[t], []
    for _ in range(8):
        rows.append(jnp.concatenate([qr, qi], axis=-1))
        qr, qi = qr * qr - qi * qi, 2.0 * qr * qi
    apow = jnp.stack(rows, axis=1)
    d_row = jnp.tile(d_skip.astype(F32), (1, t)).reshape(S5_GROUPS, 1, S5_ROW)
    b_pad = jnp.zeros((8, 512), F32).at[0].set(b_glu.astype(F32))
    return (mt.astype(BF16), bend.astype(BF16), cpow.astype(BF16), apow, d_row,
            w_glu.astype(BF16), b_pad)


def _rmsnorm_bf16(x, gain):
    ms = jnp.mean(x * x, axis=-1, keepdims=True)
    return (x * lax.rsqrt(ms + NORM_EPS) * gain).astype(BF16)


def _ffn_kernel(x_ref, g_ref, wg_ref, wu_ref, wo_ref, o_ref, h_sc, acc_sc):
    f = pl.program_id(1)

    @pl.when(f == 0)
    def _():
        h_sc[...] = _rmsnorm_bf16(x_ref[...], g_ref[0:1, :])
        acc_sc[...] = jnp.zeros_like(acc_sc)

    h = h_sc[...]
    gate = jnp.dot(h, wg_ref[...], preferred_element_type=F32)
    up = jnp.dot(h, wu_ref[...], preferred_element_type=F32)
    act = (_silu(gate) * up).astype(BF16)
    acc_sc[...] += jnp.dot(act, wo_ref[...], preferred_element_type=F32)

    @pl.when(f == pl.num_programs(1) - 1)
    def _():
        o_ref[...] = x_ref[...] + FFN_RES * acc_sc[...]


def _ffn(x, gain, w_in, w_out, layer, *, tm=512, tf=512):
    m, d = x.shape
    d_ff = w_out.shape[1]
    tm = min(tm, m)
    nf = d_ff // tf
    return pl.pallas_call(
        _ffn_kernel,
        grid=(m // tm, nf),
        in_specs=[
            pl.BlockSpec((tm, d), lambda i, f: (i, 0)),
            pl.BlockSpec((8, d), lambda i, f: (0, 0)),
            pl.BlockSpec((None, d, tf), lambda i, f: (layer, 0, f)),
            pl.BlockSpec((None, d, tf), lambda i, f: (layer, 0, nf + f)),
            pl.BlockSpec((None, tf, d), lambda i, f: (layer, f, 0)),
        ],
        out_specs=pl.BlockSpec((tm, d), lambda i, f: (i, 0)),
        out_shape=jax.ShapeDtypeStruct((m, d), F32),
        scratch_shapes=[pltpu.VMEM((tm, d), BF16), pltpu.VMEM((tm, d), F32)],
        compiler_params=_cparams(("parallel", "arbitrary")),
        name="ffn",
    )(x, gain, w_in, w_in, w_out)


def _proj_kernel(x_ref, g_ref, w_rw_ref, w_gla_ref, w_fox_ref, w_s5_ref, rw_ref, gla_ref, fox_ref, s5_ref):
    h = _rmsnorm_bf16(x_ref[...], g_ref[0:1, :])
    rw_ref[...] = jnp.dot(h, w_rw_ref[...], preferred_element_type=F32)
    gla_ref[...] = jnp.dot(h, w_gla_ref[...], preferred_element_type=F32)
    fox_ref[...] = jnp.dot(h, w_fox_ref[...], preferred_element_type=F32)
    s5_ref[...] = jnp.dot(h, w_s5_ref[...], preferred_element_type=F32).astype(s5_ref.dtype)


def _proj(x, gain, w_rw, w_gla, w_fox, w_s5, *, tm=256):
    m, d = x.shape
    tm = min(tm, m)
    widths = (RW_COLS, GLA_PACKED, FOX_PACKED, 512)
    resident = lambda n: pl.BlockSpec((d, n), lambda i: (0, 0), pipeline_mode=pl.Buffered(1))
    return pl.pallas_call(
        _proj_kernel,
        grid=(m // tm,),
        in_specs=[pl.BlockSpec((tm, d), lambda i: (i, 0)), pl.BlockSpec((8, d), lambda i: (0, 0))]
                 + [resident(n) for n in widths],
        out_specs=[pl.BlockSpec((tm, n), lambda i: (i, 0)) for n in widths],
        out_shape=[jax.ShapeDtypeStruct((m, n), F32) for n in widths[:3]]
                  + [jax.ShapeDtypeStruct((m, widths[3]), BF16)],
        compiler_params=_cparams(("parallel",)),
        name="mixer_in_proj",
    )(x, gain, w_rw, w_gla, w_fox, w_s5)


def _merge_kernel(x_ref, g_ref, y0_ref, y1_ref, y2_ref, y3_ref, wg_ref, wb_ref, wo_ref, o_ref, h_sc, acc_sc):
    n = pl.program_id(1)

    @pl.when(n == 0)
    def _():
        h_sc[...] = _rmsnorm_bf16(x_ref[...], g_ref[0:1, :])
        acc_sc[...] = jnp.zeros_like(acc_sc)

    h = h_sc[...]
    merged = None
    for j, y_ref in enumerate((y0_ref, y1_ref, y2_ref, y3_ref)):
        gate = _sigmoid(jnp.dot(h, wg_ref[j], preferred_element_type=F32))
        term = gate * jnp.dot(y_ref[...], wb_ref[j], preferred_element_type=F32)
        merged = term if merged is None else merged + term
    acc_sc[...] += jnp.dot(merged.astype(BF16), wo_ref[...], preferred_element_type=F32)

    @pl.when(n == pl.num_programs(1) - 1)
    def _():
        o_ref[...] = x_ref[...] + acc_sc[...]


def _merge(x, gain, ys, w_gate, w_branch, w_o, layer, *, tm=512, tn=512):
    m, d = x.shape
    tm = min(tm, m)
    bw = ys[0].shape[1]
    return pl.pallas_call(
        _merge_kernel,
        grid=(m // tm, d // tn),
        in_specs=[pl.BlockSpec((tm, d), lambda i, n: (i, 0)), pl.BlockSpec((8, d), lambda i, n: (0, 0))]
                 + [pl.BlockSpec((tm, bw), lambda i, n: (i, 0)) for _ in ys]
                 + [pl.BlockSpec((None, 4, d, tn), lambda i, n: (layer, 0, 0, n)),
                    pl.BlockSpec((None, 4, bw, tn), lambda i, n: (layer, 0, 0, n)),
                    pl.BlockSpec((None, tn, d), lambda i, n: (layer, n, 0))],
        out_specs=pl.BlockSpec((tm, d), lambda i, n: (i, 0)),
        out_shape=jax.ShapeDtypeStruct((m, d), F32),
        scratch_shapes=[pltpu.VMEM((tm, d), BF16), pltpu.VMEM((tm, d), F32)],
        compiler_params=_cparams(("parallel", "arbitrary")),
        name="merge",
    )(x, gain, *ys, w_gate, w_branch, w_o)


def _gain_rows(g):
    return jnp.broadcast_to(g.astype(F32)[None, :], (8, g.shape[0]))


def _pack_w_in(w):
    d = w.shape[0]
    o = 0
    w_rw = w[:, o:o + RW_COLS]
    o += RW_COLS
    q, k, v, glr, og = (w[:, o:o + 256], w[:, o + 256:o + 512], w[:, o + 512:o + 1024],
                        w[:, o + 1024:o + 1040], w[:, o + 1040:o + 1552])
    w_gla = jnp.concatenate([q, k, v, og, glr, jnp.zeros((d, 128 - GLA_GATE_RANK), w.dtype)], axis=1)
    o += 1552
    w_fox = jnp.concatenate([w[:, o:o + 1544], jnp.zeros((d, 128 - FOX_HEADS), w.dtype)], axis=1)
    o += 1544
    w_s5 = w[:, o:o + 512]
    return [t.astype(BF16) for t in (w_rw, w_gla, w_fox, w_s5)]


def kernel(x, norm_ffa, w_ffa_in, w_ffa_out, norm_mix, w_in, rw_mu, rw_w0, rw_w2, rw_a0, rw_a2, rw_g2, rw_kk, rw_ka, rw_rk, rw_ln_g, rw_ln_b, gla_wg2, gla_bg2, gla_norm_g, fox_bf, fox_qg, fox_kg, s5_lam_re, s5_lam_im, s5_log_dt, s5_b_re, s5_b_im, s5_c_re, s5_c_im, s5_d, s5_w_glu, s5_b_glu, w_gate, w_branch, w_o, norm_ffb, w_ffb_in, w_ffb_out):
    b, l, d = x.shape
    m = b * l
    x = x.reshape(m, d).astype(F32)
    w_ffa_in, w_ffa_out, w_ffb_in, w_ffb_out, w_gate, w_branch, w_o = (
        w.astype(BF16) for w in (w_ffa_in, w_ffa_out, w_ffb_in, w_ffb_out, w_gate, w_branch, w_o))
    for i in range(norm_ffa.shape[0]):
        x = _ffn(x, _gain_rows(norm_ffa[i]), w_ffa_in, w_ffa_out, i)
        gain_mix = _gain_rows(norm_mix[i])
        c_rw, c_gla, c_fox, c_s5 = _proj(x, gain_mix, *_pack_w_in(w_in[i]))
        y_rw = _rwkv7(c_rw.reshape(b, l, -1),
                      *_rwkv7_params(rw_mu[i], rw_w0[i], rw_w2[i], rw_a0[i], rw_a2[i], rw_g2[i],
                                     rw_kk[i], rw_ka[i], rw_rk[i], rw_ln_g[i], rw_ln_b[i]))
        y_gla = _gla(c_gla.reshape(b, l, -1), *_gla_params(gla_wg2[i], gla_bg2[i], gla_norm_g[i]))
        y_fox = _fox(c_fox.reshape(b, l, -1), *_fox_params(fox_bf[i], fox_qg[i], fox_kg[i]))
        y_s5 = _s5(c_s5.reshape(b, l, -1),
                   *_s5_params(s5_lam_re[i], s5_lam_im[i], s5_log_dt[i], s5_b_re[i], s5_b_im[i],
                               s5_c_re[i], s5_c_im[i], s5_d[i], s5_w_glu[i], s5_b_glu[i]))
        ys = [y.reshape(m, -1) for y in (y_rw, y_gla, y_fox, y_s5)]
        x = _merge(x, gain_mix, ys, w_gate, w_branch, w_o, i)
        x = _ffn(x, _gain_rows(norm_ffb[i]), w_ffb_in, w_ffb_out, i)
    return x.reshape(b, l, d)
```
